```python
import jax, jax.numpy as jnp
from jax import lax
import numpy as np

D_MODEL = 2048
BATCH = 16
SEQ = 2048
DEPTH = 4

N_BRANCH = 4
BRANCH_W = D_MODEL // 4
N_GROUPS = 4
GROUP_W = BRANCH_W // N_GROUPS
CHUNK = 128
SCONV_W = 3
N_HEADS = N_GROUPS
HEAD_DIM = GROUP_W
MOBA_BLOCK = 256
MOBA_TOPK = 3
Q_BLOCK = 128
ROPE_THETA = 10000.0
NEG = -1e30
LRU_CONV_W = 4
LRU_C = 8.0
D_FF = (((8 * D_MODEL + 2) // 3 + 255) // 256) * 256
EPS = 1e-6

A_OFF = 0
B_OFF = 2 * BRANCH_W
C_OFF = 5 * BRANCH_W
D_OFF = 8 * BRANCH_W
IN_W = 10 * BRANCH_W

kernel_name = "hybrid_gated_parallel_mixers"


def rmsnorm(x, g):
    xf = x.astype(jnp.float32)
    var = jnp.mean(xf * xf, axis=-1, keepdims=True)
    return (xf * lax.rsqrt(var + EPS)).astype(x.dtype) * g


def causal_depthwise_conv(x, w):
    K = w.shape[0]
    S = x.shape[1]
    xp = jnp.pad(x, ((0, 0), (K - 1, 0), (0, 0)))
    return sum(xp[:, k:k + S] * w[k] for k in range(K))


def rope(x, pos):
    half = x.shape[-1] // 2
    inv = ROPE_THETA ** (-jnp.arange(half, dtype=jnp.float32) / half)
    ang = pos.astype(jnp.float32)[:, None] * inv[None, :]
    cos = jnp.cos(ang)[None, :, None, :]
    sin = jnp.sin(ang)[None, :, None, :]
    xf = x.astype(jnp.float32)
    x1, x2 = xf[..., :half], xf[..., half:]
    return jnp.concatenate([x1 * cos - x2 * sin, x2 * cos + x1 * sin], axis=-1).astype(x.dtype)


def spatial_gating(u, v, w_s, b_s, g_v):
    Bn, S, _ = u.shape
    v = rmsnorm(v, g_v)
    mask = jnp.tril(jnp.ones((CHUNK, CHUNK), dtype=bool))
    w = jnp.where(mask[None], w_s, 0)
    vc = v.reshape(Bn, S // CHUNK, CHUNK, N_GROUPS, GROUP_W)
    mix = jnp.einsum('gts,bnsgc->bntgc', w, vc) + b_s[None, None, :, :, None]
    return u * mix.reshape(Bn, S, BRANCH_W)


def short_conv_mixer(b_gate, c_gate, xc, w_conv):
    return b_gate * causal_depthwise_conv(c_gate * xc, w_conv)


def moba_attention(q, k, v):
    Bn, S, H, Dh = q.shape
    nkb = -(-S // MOBA_BLOCK)
    pad = nkb * MOBA_BLOCK - S
    kp = jnp.pad(k, ((0, 0), (0, pad), (0, 0), (0, 0)))
    vp = jnp.pad(v, ((0, 0), (0, pad), (0, 0), (0, 0)))
    kb = kp.reshape(Bn, nkb, MOBA_BLOCK, H, Dh).transpose(0, 3, 1, 2, 4)
    vb = vp.reshape(Bn, nkb, MOBA_BLOCK, H, Dh).transpose(0, 3, 1, 2, 4)
    kmean = jnp.mean(kb.astype(jnp.float32), axis=3)
    own = jnp.arange(S) // MOBA_BLOCK
    gate = jnp.einsum('bshd,bhnd->bhsn', q.astype(jnp.float32), kmean)
    past = jnp.arange(nkb)[None, :] < own[:, None]
    gate = jnp.where(past[None, None], gate, NEG)
    topk = min(MOBA_TOPK, nkb)
    _, idx = lax.top_k(gate, topk)
    valid = idx < own[None, None, :, None]

    nqb = S // Q_BLOCK
    qs = q.reshape(Bn, nqb, Q_BLOCK, H, Dh).transpose(0, 1, 3, 2, 4).reshape(Bn * nqb, H, Q_BLOCK, Dh)
    idx_s = idx.reshape(Bn, H, nqb, Q_BLOCK, topk).transpose(0, 2, 1, 3, 4).reshape(Bn * nqb, H, Q_BLOCK, topk)
    val_s = valid.reshape(Bn, H, nqb, Q_BLOCK, topk).transpose(0, 2, 1, 3, 4).reshape(Bn * nqb, H, Q_BLOCK, topk)
    b_ids = jnp.repeat(jnp.arange(Bn), nqb)
    qb_ids = jnp.tile(jnp.arange(nqb), Bn)
    scale = Dh ** -0.5
    L = MOBA_BLOCK

    def one_block(args):
        qblk, idb, vld, b, qb = args
        kbb = kb[b]
        vbb = vb[b]
        kg = jax.vmap(lambda kh, ih: kh[ih])(kbb, idb)
        vg = jax.vmap(lambda vh, ih: vh[ih])(vbb, idb)
        s_sel = jnp.einsum('hqd,hqjld->hqjl', qblk, kg).astype(jnp.float32) * scale
        s_sel = jnp.where(vld[..., None], s_sel, NEG)
        qpos = qb * Q_BLOCK + jnp.arange(Q_BLOCK)
        own_blk = (qb * Q_BLOCK) // MOBA_BLOCK
        k_own = lax.dynamic_index_in_dim(kbb, own_blk, axis=1, keepdims=False)
        v_own = lax.dynamic_index_in_dim(vbb, own_blk, axis=1, keepdims=False)
        s_own = jnp.einsum('hqd,hld->hql', qblk, k_own).astype(jnp.float32) * scale
        kpos = own_blk * MOBA_BLOCK + jnp.arange(MOBA_BLOCK)
        s_own = jnp.where(kpos[None, None, :] <= qpos[None, :, None], s_own, NEG)
        s_all = jnp.concatenate([s_sel.reshape(H, Q_BLOCK, topk * L), s_own], axis=-1)
        p = jax.nn.softmax(s_all, axis=-1).astype(v.dtype)
        p_sel = p[..., :topk * L].reshape(H, Q_BLOCK, topk, L)
        p_own = p[..., topk * L:]
        return (jnp.einsum('hqjl,hqjld->hqd', p_sel, vg)
                + jnp.einsum('hql,hld->hqd', p_own, v_own))

    out = lax.map(one_block, (qs, idx_s, val_s, b_ids, qb_ids))
    return out.reshape(Bn, nqb, H, Q_BLOCK, Dh).transpose(0, 1, 3, 2, 4).reshape(Bn, S, H * Dh)


def rg_lru_mixer(xr, gate_in, conv_w, conv_b, w_a, b_a, w_x, b_x, lam):
    Bn, S, _ = xr.shape
    xr = causal_depthwise_conv(xr, conv_w) + conv_b
    xg = xr.reshape(Bn, S, N_GROUPS, GROUP_W)
    r = jax.nn.sigmoid(jnp.einsum('bsgi,gij->bsgj', xg, w_a).reshape(Bn, S, BRANCH_W) + b_a)
    i = jax.nn.sigmoid(jnp.einsum('bsgi,gij->bsgj', xg, w_x).reshape(Bn, S, BRANCH_W) + b_x)
    log_a = -LRU_C * r.astype(jnp.float32) * jax.nn.softplus(-lam.astype(jnp.float32))
    a = jnp.exp(log_a)
    bterm = jnp.sqrt(-jnp.expm1(2.0 * log_a)) * (i * xr).astype(jnp.float32)

    def comb(c1, c2):
        a1, b1 = c1
        a2, b2 = c2
        return a1 * a2, a2 * b1 + b2

    _, h = lax.associative_scan(comb, (a, bterm), axis=1)
    return jax.nn.gelu(gate_in) * h.astype(xr.dtype)


def setup_inputs(seed: int = 0) -> dict:
    key = jax.random.key(seed)
    ks = iter(jax.random.split(key, 32))

    def nrm(shape, scale):
        return jax.random.normal(next(ks), shape, jnp.float32) * scale

    def gain(shape):
        return 1.0 + 0.05 * jax.random.normal(next(ks), shape, jnp.float32)

    out_scale = (2 * DEPTH) ** -0.5
    x = jax.random.normal(next(ks), (BATCH, SEQ, D_MODEL), jnp.float32)
    u = jax.random.uniform(next(ks), (DEPTH, BRANCH_W), jnp.float32, 0.9, 0.999)
    a0 = u ** (1.0 / LRU_C)
    lru_lambda = jnp.log(a0) - jnp.log1p(-a0)
    return {
        "x": x,
        "g_mix": gain((DEPTH, D_MODEL)),
        "w_in": nrm((DEPTH, D_MODEL, IN_W), D_MODEL ** -0.5),
        "w_sgu": nrm((DEPTH, N_GROUPS, CHUNK, CHUNK), CHUNK ** -0.5),
        "b_sgu": gain((DEPTH, CHUNK, N_GROUPS)),
        "g_sgu": gain((DEPTH, BRANCH_W)),
        "w_sconv": nrm((DEPTH, SCONV_W, BRANCH_W), SCONV_W ** -0.5),
        "w_lru_conv": nrm((DEPTH, LRU_CONV_W, BRANCH_W), LRU_CONV_W ** -0.5),
        "b_lru_conv": nrm((DEPTH, BRANCH_W), 0.01),
        "w_lru_a": nrm((DEPTH, N_GROUPS, GROUP_W, GROUP_W), GROUP_W ** -0.5),
        "b_lru_a": nrm((DEPTH, BRANCH_W), 0.01),
        "w_lru_x": nrm((DEPTH, N_GROUPS, GROUP_W, GROUP_W), GROUP_W ** -0.5),
        "b_lru_x": nrm((DEPTH, BRANCH_W), 0.01),
        "lru_lambda": lru_lambda,
        "w_gate": nrm((DEPTH, N_BRANCH, D_MODEL, D_MODEL), D_MODEL ** -0.5),
        "b_gate": nrm((DEPTH, N_BRANCH, D_MODEL), 0.01),
        "w_branch": nrm((DEPTH, N_BRANCH, BRANCH_W, D_MODEL), BRANCH_W ** -0.5),
        "w_out": nrm((DEPTH, D_MODEL, D_MODEL), D_MODEL ** -0.5 * out_scale),
        "g_ffn": gain((DEPTH, D_MODEL)),
        "w_ffn1": nrm((DEPTH, D_MODEL, D_FF), D_MODEL ** -0.5),
        "w_ffn3": nrm((DEPTH, D_MODEL, D_FF), D_MODEL ** -0.5),
        "w_ffn2": nrm((DEPTH, D_FF, D_MODEL), D_FF ** -0.5 * out_scale),
        "g_final": gain((D_MODEL,)),
    }


def reference(x, g_mix, w_in, w_sgu, b_sgu, g_sgu, w_sconv, w_lru_conv, b_lru_conv,
              w_lru_a, b_lru_a, w_lru_x, b_lru_x, lru_lambda, w_gate, b_gate, w_branch,
              w_out, g_ffn, w_ffn1, w_ffn3, w_ffn2, g_final):
    Bn, S, _ = x.shape
    pos = jnp.arange(S)
    BW = BRANCH_W
    for l in range(DEPTH):
        xn = rmsnorm(x, g_mix[l])
        p = xn @ w_in[l]
        ua = jax.nn.gelu(p[..., A_OFF:A_OFF + BW])
        va = jax.nn.gelu(p[..., A_OFF + BW:A_OFF + 2 * BW])
        o_a = spatial_gating(ua, va, w_sgu[l], b_sgu[l], g_sgu[l])
        o_b = short_conv_mixer(p[..., B_OFF:B_OFF + BW], p[..., B_OFF + BW:B_OFF + 2 * BW],
                               p[..., B_OFF + 2 * BW:B_OFF + 3 * BW], w_sconv[l])
        q = rope(p[..., C_OFF:C_OFF + BW].reshape(Bn, S, N_HEADS, HEAD_DIM), pos)
        k = rope(p[..., C_OFF + BW:C_OFF + 2 * BW].reshape(Bn, S, N_HEADS, HEAD_DIM), pos)
        v = p[..., C_OFF + 2 * BW:C_OFF + 3 * BW].reshape(Bn, S, N_HEADS, HEAD_DIM)
        o_c = moba_attention(q, k, v)
        o_d = rg_lru_mixer(p[..., D_OFF:D_OFF + BW], p[..., D_OFF + BW:D_OFF + 2 * BW],
                           w_lru_conv[l], b_lru_conv[l], w_lru_a[l], b_lru_a[l],
                           w_lru_x[l], b_lru_x[l], lru_lambda[l])
        y = 0.0
        for bi, o in enumerate((o_a, o_b, o_c, o_d)):
            g = jax.nn.sigmoid(xn @ w_gate[l, bi] + b_gate[l, bi])
            y = y + g * (o @ w_branch[l, bi])
        x = x + y @ w_out[l]
        xn = rmsnorm(x, g_ffn[l])
        h = jax.nn.silu(xn @ w_ffn1[l]) * (xn @ w_ffn3[l])
        x = x + h @ w_ffn2[l]
    return rmsnorm(x, g_final)
```

```python
import functools
import math

import jax
import jax.numpy as jnp
import numpy as np
from jax import lax
from jax.experimental import pallas as pl
from jax.experimental.pallas import tpu as pltpu

N_BRANCH = 4
N_GROUPS = 4
CHUNK = 128
SCONV_W = 3
MOBA_BLOCK = 256
MOBA_TOPK = 3
ROPE_THETA = 10000.0
NEG = -1e30
LRU_CONV_W = 4
LRU_C = 8.0
EPS = 1e-6

LANES = 128
V7X_VMEM_BYTES = 64 * 1024 * 1024
VMEM_BUDGET_BYTES = 56 * 1024 * 1024

F32 = jnp.float32
BF16 = jnp.bfloat16

_SQRT_2_OVER_PI = float(np.sqrt(2.0 / np.pi).astype(np.float32))
_LOG2_E = 1.4426950408889634


def _params(semantics, vmem_bytes):
    assert vmem_bytes <= VMEM_BUDGET_BYTES, vmem_bytes
    return pltpu.CompilerParams(dimension_semantics=semantics,
                                vmem_limit_bytes=VMEM_BUDGET_BYTES)


def _nbytes(shape, dtype):
    return int(np.prod(shape)) * jnp.dtype(dtype).itemsize


def _rmsnorm(x, g):
    var = jnp.mean(x * x, axis=-1, keepdims=True)
    return x * lax.rsqrt(var + EPS) * g


def _gelu_tanh(x):
    return x * (0.5 * (1.0 + jnp.tanh(_SQRT_2_OVER_PI * (x + 0.044715 * (x * x * x)))))


def _sigmoid(x):
    return 0.5 * (jnp.tanh(0.5 * x) + 1.0)


def _softplus(x):
    return jnp.maximum(x, 0.0) + jnp.log1p(jnp.exp(-jnp.abs(x)))


def _shift_rows(x, d):
    row = lax.broadcasted_iota(jnp.int32, x.shape, 0)
    return jnp.where(row >= d, pltpu.roll(x, d, axis=0), 0.0)


def _causal_conv(x, w_ref, width):
    acc = None
    for k in range(width):
        d = width - 1 - k
        term = (_shift_rows(x, d) if d else x) * w_ref[k:k + 1, :]
        acc = term if acc is None else acc + term
    return acc


def _linear_scan(a, b):
    n = a.shape[0]
    row = lax.broadcasted_iota(jnp.int32, a.shape, 0)
    d = 1
    while d < n:
        keep = row >= d
        b = a * jnp.where(keep, pltpu.roll(b, d, axis=0), 0.0) + b
        if 2 * d < n:
            a = a * jnp.where(keep, pltpu.roll(a, d, axis=0), 1.0)
        d *= 2
    return b


def _dot(a, b):
    return jnp.dot(a, b, preferred_element_type=F32)


def _dot_nt(a, b):
    return lax.dot_general(a, b, (((1,), (1,)), ((), ())), preferred_element_type=F32)


def _split_bf16(x):
    hi = x.astype(BF16)
    lo = (x - hi.astype(F32)).astype(BF16)
    return hi, lo


def _in_proj_kernel(x_ref, g_ref, w_ref, p_ref, xn_ref):
    @pl.when(pl.program_id(1) == 0)
    def _():
        xn_ref[...] = _rmsnorm(x_ref[...], g_ref[...]).astype(BF16)

    p_ref[...] = _dot(xn_ref[...], w_ref[...])


def _in_proj(x, g, w, *, tm=1024, tn=512):
    m, d = x.shape
    n = w.shape[1]
    vmem = (2 * _nbytes((tm, d), F32) + 2 * _nbytes((d, tn), BF16)
            + 2 * _nbytes((tm, tn), F32) + _nbytes((tm, d), BF16))
    return pl.pallas_call(
        _in_proj_kernel,
        grid=(m // tm, n // tn),
        in_specs=[
            pl.BlockSpec((tm, d), lambda i, j: (i, 0)),
            pl.BlockSpec((1, d), lambda i, j: (0, 0)),
            pl.BlockSpec((d, tn), lambda i, j: (0, j)),
        ],
        out_specs=pl.BlockSpec((tm, tn), lambda i, j: (i, j)),
        out_shape=jax.ShapeDtypeStruct((m, n), F32),
        scratch_shapes=[pltpu.VMEM((tm, d), BF16)],
        compiler_params=_params(("parallel", "arbitrary"), vmem),
        name="in_proj",
    )(x, g, w)


def _sgu_kernel(u_ref, v_ref, w_ref, b_ref, g_ref, o_ref):
    rows = u_ref.shape[0]
    v = _rmsnorm(_gelu_tanh(v_ref[...]), g_ref[...]).astype(BF16)
    r = lax.broadcasted_iota(jnp.int32, (CHUNK, CHUNK), 0)
    c = lax.broadcasted_iota(jnp.int32, (CHUNK, CHUNK), 1)
    causal = c <= r
    for g in range(N_GROUPS):
        cs = slice(g * LANES, (g + 1) * LANES)
        w = jnp.where(causal, w_ref[g], 0.0).astype(BF16)
        bias = b_ref[g]
        for n in range(rows // CHUNK):
            rs = slice(n * CHUNK, (n + 1) * CHUNK)
            mix = _dot(w, v[rs, cs]) + bias
            o_ref[rs, cs] = (_gelu_tanh(u_ref[rs, cs]) * mix).astype(BF16)


def _sgu(p, w, b, g, *, width, ts=512):
    m = p.shape[0]
    vmem = 4 * _nbytes((ts, width), F32) + 2 * _nbytes((ts, width), BF16)
    return pl.pallas_call(
        _sgu_kernel,
        grid=(m // ts,),
        in_specs=[
            pl.BlockSpec((ts, width), lambda i: (i, 0)),
            pl.BlockSpec((ts, width), lambda i: (i, 1)),
            pl.BlockSpec(w.shape, lambda i: (0, 0, 0)),
            pl.BlockSpec(b.shape, lambda i: (0, 0, 0)),
            pl.BlockSpec((1, width), lambda i: (0, 0)),
        ],
        out_specs=pl.BlockSpec((ts, width), lambda i: (i, 0)),
        out_shape=jax.ShapeDtypeStruct((m, width), BF16),
        compiler_params=_params(("parallel",), vmem),
        name="sgu",
    )(p, p, w, b, g)


def _sconv_kernel(bg_ref, cg_ref, x_ref, w_ref, o_ref):
    z = cg_ref[...] * x_ref[...]
    o_ref[...] = (bg_ref[...] * _causal_conv(z, w_ref, SCONV_W)).astype(BF16)


def _sconv(p, w, *, batch, seq, col0):
    m = p.shape[0]
    width = w.shape[1]
    ng = width // LANES
    blk = (seq, LANES)
    vmem = 6 * _nbytes(blk, F32) + 2 * _nbytes(blk, BF16)
    return pl.pallas_call(
        _sconv_kernel,
        grid=(batch, ng),
        in_specs=[
            pl.BlockSpec(blk, lambda b, g: (b, col0 + g)),
            pl.BlockSpec(blk, lambda b, g: (b, col0 + ng + g)),
            pl.BlockSpec(blk, lambda b, g: (b, col0 + 2 * ng + g)),
            pl.BlockSpec((SCONV_W, LANES), lambda b, g: (0, g)),
        ],
        out_specs=pl.BlockSpec(blk, lambda b, g: (b, g)),
        out_shape=jax.ShapeDtypeStruct((m, width), BF16),
        compiler_params=_params(("parallel", "parallel"), vmem),
        name="sconv",
    )(p, p, p, w)


def _rope(x, cos, sin_signed):
    return x * cos + pltpu.roll(x, x.shape[1] // 2, axis=1) * sin_signed


def _moba_kernel(q_ref, k_ref, v_ref, cos_ref, sin_ref, o_ref,
                 kr_ref, vb_ref, km_ref, s_ref):
    seq, dh = q_ref.shape
    blk = MOBA_BLOCK
    nblk = seq // blk
    exp2_scale = dh ** -0.5 * _LOG2_E

    km_ref[...] = jnp.zeros(km_ref.shape, F32)
    for n in range(nblk):
        rs = slice(n * blk, (n + 1) * blk)
        kr = _rope(k_ref[rs, :], cos_ref[rs, :], sin_ref[rs, :])
        kr_ref[rs, :] = kr.astype(BF16)
        km_ref[n:n + 1, :] = jnp.sum(kr, axis=0, keepdims=True) * (1.0 / blk)
        vb_ref[rs, :] = v_ref[rs, :].astype(BF16)
    km_hi, km_lo = _split_bf16(km_ref[...])

    row = lax.broadcasted_iota(jnp.int32, (blk, blk), 0)
    col = lax.broadcasted_iota(jnp.int32, (blk, blk), 1)
    causal = col <= row
    lane = lax.broadcasted_iota(jnp.int32, (blk, km_ref.shape[0]), 1)

    for i in range(nblk):
        rs = slice(i * blk, (i + 1) * blk)
        qr = _rope(q_ref[rs, :], cos_ref[rs, :], sin_ref[rs, :])
        qb = qr.astype(BF16)

        choose = i > MOBA_TOPK
        if choose:
            q_lo = (qr - qb.astype(F32)).astype(BF16)
            gate = _dot_nt(qb, km_hi) + (_dot_nt(qb, km_lo) + _dot_nt(q_lo, km_hi))
            cnt = jnp.zeros(gate.shape, F32)
            for m in range(i):
                gm = gate[:, m:m + 1]
                beats = (gm > gate) | ((gm == gate) & (lane > m))
                cnt = cnt + jnp.where(beats, 1.0, 0.0)
            sel = jnp.where(cnt < float(MOBA_TOPK), 1.0, 0.0)

        m_run = None
        for j in range(i + 1):
            ks = slice(j * blk, (j + 1) * blk)
            s = _dot_nt(qb, kr_ref[ks, :])
            if j == i:
                s = jnp.where(causal, s, NEG)
            elif choose:
                s = jnp.where(sel[:, j:j + 1] > 0.5, s, NEG)
            s_ref[:, ks] = s
            mj = jnp.max(s, axis=-1, keepdims=True)
            m_run = mj if m_run is None else jnp.maximum(m_run, mj)

        l_run = jnp.zeros((blk, 1), F32)
        acc = jnp.zeros((blk, dh), F32)
        for j in range(i + 1):
            ks = slice(j * blk, (j + 1) * blk)
            p = jnp.exp2((s_ref[:, ks] - m_run) * exp2_scale)
            l_run = l_run + jnp.sum(p, axis=-1, keepdims=True)
            acc = acc + _dot(p.astype(BF16), vb_ref[ks, :])
        o_ref[rs, :] = (acc / l_run).astype(BF16)


def _moba(p, cos, sin_signed, *, batch, seq, heads, col0):
    m = p.shape[0]
    dh = LANES
    blk = (seq, dh)
    vmem = (6 * _nbytes(blk, F32) + 4 * _nbytes(blk, F32) + 2 * _nbytes(blk, BF16)
            + 2 * _nbytes(blk, BF16) + _nbytes((LANES, dh), F32)
            + _nbytes((MOBA_BLOCK, seq), F32))
    return pl.pallas_call(
        _moba_kernel,
        grid=(batch, heads),
        in_specs=[
            pl.BlockSpec(blk, lambda b, h: (b, col0 + h)),
            pl.BlockSpec(blk, lambda b, h: (b, col0 + heads + h)),
            pl.BlockSpec(blk, lambda b, h: (b, col0 + 2 * heads + h)),
            pl.BlockSpec(blk, lambda b, h: (0, 0)),
            pl.BlockSpec(blk, lambda b, h: (0, 0)),
        ],
        out_specs=pl.BlockSpec(blk, lambda b, h: (b, h)),
        out_shape=jax.ShapeDtypeStruct((m, heads * dh), BF16),
        scratch_shapes=[
            pltpu.VMEM(blk, BF16),
            pltpu.VMEM(blk, BF16),
            pltpu.VMEM((LANES, dh), F32),
            pltpu.VMEM((MOBA_BLOCK, seq), F32),
        ],
        compiler_params=_params(("parallel", "parallel"), vmem),
        name="moba",
    )(p, p, p, cos, sin_signed)


def _lru_kernel(xr_ref, gate_ref, cw_ref, cb_ref, wa_ref, ba_ref, wx_ref, bx_ref,
                lam_ref, o_ref):
    xc = _causal_conv(xr_ref[...], cw_ref, LRU_CONV_W) + cb_ref[...]
    xb = xc.astype(BF16)
    r = _sigmoid(_dot(xb, wa_ref[0]) + ba_ref[...])
    i = _sigmoid(_dot(xb, wx_ref[0]) + bx_ref[...])
    log_a = (-LRU_C * r) * _softplus(-lam_ref[...])
    a = jnp.exp(log_a)
    one_minus_a2 = jnp.tanh(-log_a) * (1.0 + a * a)
    b = jnp.sqrt(one_minus_a2) * (i * xc)
    h = _linear_scan(a, b)
    o_ref[...] = (_gelu_tanh(gate_ref[...]) * h).astype(BF16)


def _lru(p, cw, cb, wa, ba, wx, bx, lam, *, batch, seq, col0):
    m = p.shape[0]
    width = cw.shape[1]
    ng = width // LANES
    blk = (seq, LANES)
    vec = pl.BlockSpec((1, LANES), lambda b, g: (0, g))
    mat = pl.BlockSpec((1, LANES, LANES), lambda b, g: (g, 0, 0))
    vmem = 4 * _nbytes(blk, F32) + 2 * _nbytes(blk, BF16) + 12 * _nbytes(blk, F32)
    return pl.pallas_call(
        _lru_kernel,
        grid=(batch, ng),
        in_specs=[
            pl.BlockSpec(blk, lambda b, g: (b, col0 + g)),
            pl.BlockSpec(blk, lambda b, g: (b, col0 + ng + g)),
            pl.BlockSpec((LRU_CONV_W, LANES), lambda b, g: (0, g)),
            vec, mat, vec, mat, vec, vec,
        ],
        out_specs=pl.BlockSpec(blk, lambda b, g: (b, g)),
        out_shape=jax.ShapeDtypeStruct((m, width), BF16),
        compiler_params=_params(("parallel", "parallel"), vmem),
        name="rg_lru",
    )(p, p, cw, cb, wa, ba, wx, bx, lam)


def _merge_kernel(x_ref, g_ref, oa_ref, ob_ref, oc_ref, od_ref, wg_ref, bg_ref, wb_ref,
                  wo_ref, out_ref, xn_ref, acc_ref):
    j = pl.program_id(1)

    @pl.when(j == 0)
    def _():
        xn_ref[...] = _rmsnorm(x_ref[...], g_ref[...]).astype(BF16)
        acc_ref[...] = jnp.zeros(acc_ref.shape, F32)

    xn = xn_ref[...]
    y = None
    for k, o_ref in enumerate((oa_ref, ob_ref, oc_ref, od_ref)):
        gate = _sigmoid(_dot(xn, wg_ref[k]) + bg_ref[k])
        t = gate * _dot(o_ref[...], wb_ref[k])
        y = t if y is None else y + t
    acc_ref[...] += _dot(y.astype(BF16), wo_ref[...])

    @pl.when(j == pl.num_programs(1) - 1)
    def _():
        out_ref[...] = x_ref[...] + acc_ref[...]


def _merge(x, g, outs, wg, bg, wb, wo, *, tm=512, tn=256):
    m, d = x.shape
    bw = outs[0].shape[1]
    nb = len(outs)
    vmem = (4 * _nbytes((tm, d), F32) + 2 * nb * _nbytes((tm, bw), BF16)
            + 2 * _nbytes((nb, d, tn), BF16) + 2 * _nbytes((nb, bw, tn), BF16)
            + 2 * _nbytes((tn, d), BF16) + _nbytes((tm, d), BF16) + _nbytes((tm, d), F32))
    o_spec = pl.BlockSpec((tm, bw), lambda i, j: (i, 0))
    return pl.pallas_call(
        _merge_kernel,
        grid=(m // tm, d // tn),
        in_specs=[
            pl.BlockSpec((tm, d), lambda i, j: (i, 0)),
            pl.BlockSpec((1, d), lambda i, j: (0, 0)),
            o_spec, o_spec, o_spec, o_spec,
            pl.BlockSpec((nb, d, tn), lambda i, j: (0, 0, j)),
            pl.BlockSpec((nb, 1, tn), lambda i, j: (0, 0, j)),
            pl.BlockSpec((nb, bw, tn), lambda i, j: (0, 0, j)),
            pl.BlockSpec((tn, d), lambda i, j: (j, 0)),
        ],
        out_specs=pl.BlockSpec((tm, d), lambda i, j: (i, 0)),
        out_shape=jax.ShapeDtypeStruct((m, d), F32),
        scratch_shapes=[pltpu.VMEM((tm, d), BF16), pltpu.VMEM((tm, d), F32)],
        compiler_params=_params(("parallel", "arbitrary"), vmem),
        name="merge",
    )(x, g, *outs, wg, bg, wb, wo)


def _ffn_kernel(*refs, final):
    if final:
        x_ref, g_ref, w1_ref, w3_ref, w2_ref, gf_ref, out_ref, xn_ref, acc_ref = refs
    else:
        x_ref, g_ref, w1_ref, w3_ref, w2_ref, out_ref, xn_ref, acc_ref = refs
    j = pl.program_id(1)

    @pl.when(j == 0)
    def _():
        xn_ref[...] = _rmsnorm(x_ref[...], g_ref[...]).astype(BF16)
        acc_ref[...] = jnp.zeros(acc_ref.shape, F32)

    xn = xn_ref[...]
    h1 = _dot(xn, w1_ref[...])
    h = (h1 * _sigmoid(h1)) * _dot(xn, w3_ref[...])
    acc_ref[...] += _dot(h.astype(BF16), w2_ref[...])

    @pl.when(j == pl.num_programs(1) - 1)
    def _():
        y = x_ref[...] + acc_ref[...]
        out_ref[...] = _rmsnorm(y, gf_ref[...]) if final else y


def _ffn(x, g, w1, w3, w2, g_final=None, *, tm=512, tf=512):
    m, d = x.shape
    f = w1.shape[1]
    final = g_final is not None
    vmem = (4 * _nbytes((tm, d), F32) + 4 * _nbytes((d, tf), BF16) + 2 * _nbytes((tf, d), BF16)
            + _nbytes((tm, d), BF16) + _nbytes((tm, d), F32))
    row = pl.BlockSpec((1, d), lambda i, j: (0, 0))
    in_specs = [
        pl.BlockSpec((tm, d), lambda i, j: (i, 0)),
        row,
        pl.BlockSpec((d, tf), lambda i, j: (0, j)),
        pl.BlockSpec((d, tf), lambda i, j: (0, j)),
        pl.BlockSpec((tf, d), lambda i, j: (j, 0)),
    ]
    args = [x, g, w1, w3, w2]
    if final:
        in_specs.append(row)
        args.append(g_final)
    return pl.pallas_call(
        functools.partial(_ffn_kernel, final=final),
        grid=(m // tm, f // tf),
        in_specs=in_specs,
        out_specs=pl.BlockSpec((tm, d), lambda i, j: (i, 0)),
        out_shape=jax.ShapeDtypeStruct((m, d), F32),
        scratch_shapes=[pltpu.VMEM((tm, d), BF16), pltpu.VMEM((tm, d), F32)],
        compiler_params=_params(("parallel", "arbitrary"), vmem),
        name="ffn_final" if final else "ffn",
    )(*args)


def _rope_tables(seq, dh):
    half = dh // 2
    inv = ROPE_THETA ** (-jnp.arange(half, dtype=F32) / half)
    ang = jnp.arange(seq).astype(F32)[:, None] * inv[None, :]
    cos, sin = jnp.cos(ang), jnp.sin(ang)
    return jnp.concatenate([cos, cos], axis=-1), jnp.concatenate([-sin, sin], axis=-1)


def kernel(x, g_mix, w_in, w_sgu, b_sgu, g_sgu, w_sconv, w_lru_conv, b_lru_conv, w_lru_a,
           b_lru_a, w_lru_x, b_lru_x, lru_lambda, w_gate, b_gate, w_branch, w_out, g_ffn,
           w_ffn1, w_ffn3, w_ffn2, g_final):
    batch, seq, d = x.shape
    depth = w_in.shape[0]
    bw = w_branch.shape[2]
    heads = bw // LANES
    assert bw == N_GROUPS * LANES and seq % MOBA_BLOCK == 0 and w_sgu.shape[-1] == CHUNK
    col_b = 2 * bw // LANES
    col_c = 5 * bw // LANES
    col_d = 8 * bw // LANES

    cos, sin_signed = _rope_tables(seq, LANES)
    row = lambda v: v.reshape(1, -1)

    h = x.reshape(batch * seq, d)
    for l in range(depth):
        p = _in_proj(h, row(g_mix[l]), w_in[l].astype(BF16))
        o_a = _sgu(p, w_sgu[l], b_sgu[l].T[:, :, None], row(g_sgu[l]), width=bw)
        o_b = _sconv(p, w_sconv[l], batch=batch, seq=seq, col0=col_b)
        o_c = _moba(p, cos, sin_signed, batch=batch, seq=seq, heads=heads, col0=col_c)
        o_d = _lru(p, w_lru_conv[l], row(b_lru_conv[l]), w_lru_a[l].astype(BF16),
                   row(b_lru_a[l]), w_lru_x[l].astype(BF16), row(b_lru_x[l]),
                   row(lru_lambda[l]), batch=batch, seq=seq, col0=col_d)
        h = _merge(h, row(g_mix[l]), (o_a, o_b, o_c, o_d), w_gate[l].astype(BF16),
                   b_gate[l][:, None, :], w_branch[l].astype(BF16), w_out[l].astype(BF16))
        h = _ffn(h, row(g_ffn[l]), w_ffn1[l].astype(BF16), w_ffn3[l].astype(BF16),
                 w_ffn2[l].astype(BF16), row(g_final) if l == depth - 1 else None)
    return h.reshape(batch, seq, d)
```

```python
import functools
import math

import jax
import jax.numpy as jnp
import numpy as np
from jax import lax
from jax.experimental import pallas as pl
from jax.experimental.pallas import tpu as pltpu

N_BRANCH = 4
N_GROUPS = 4
CHUNK = 128
SCONV_W = 3
MOBA_BLOCK = 256
MOBA_TOPK = 3
ROPE_THETA = 10000.0
NEG = -1e30
LRU_CONV_W = 4
LRU_C = 8.0
EPS = 1e-6

LANES = 128
SUBLANES = 8
V7X_VMEM_BYTES = 64 * 1024 * 1024
VMEM_BUDGET_BYTES = 56 * 1024 * 1024

F32 = jnp.float32
BF16 = jnp.bfloat16

_SQRT_2_OVER_PI = float(np.sqrt(2.0 / np.pi).astype(np.float32))
_LOG2_E = 1.4426950408889634


def _params(semantics, vmem_bytes):
    assert vmem_bytes <= VMEM_BUDGET_BYTES, vmem_bytes
    return pltpu.CompilerParams(dimension_semantics=semantics,
                                vmem_limit_bytes=VMEM_BUDGET_BYTES)


def _nbytes(shape, dtype):
    return int(np.prod(shape)) * jnp.dtype(dtype).itemsize


def _rmsnorm(x, g):
    var = jnp.mean(x * x, axis=-1, keepdims=True)
    return x * lax.rsqrt(var + EPS) * g


def _gelu_tanh(x):
    return x * (0.5 * (1.0 + jnp.tanh(_SQRT_2_OVER_PI * (x + 0.044715 * (x * x * x)))))


def _sigmoid(x):
    return 0.5 * (jnp.tanh(0.5 * x) + 1.0)


def _softplus(x):
    return jnp.maximum(x, 0.0) + jnp.log1p(jnp.exp(-jnp.abs(x)))


def _shift_rows(x, d):
    row = lax.broadcasted_iota(jnp.int32, x.shape, 0)
    return jnp.where(row >= d, pltpu.roll(x, d, axis=0), 0.0)


def _causal_conv(x, w_ref, width):
    acc = None
    for k in range(width):
        d = width - 1 - k
        term = (_shift_rows(x, d) if d else x) * w_ref[k:k + 1, :]
        acc = term if acc is None else acc + term
    return acc


def _dot(a, b):
    return jnp.dot(a, b, preferred_element_type=F32)


def _dot_nt(a, b):
    return lax.dot_general(a, b, (((1,), (1,)), ((), ())), preferred_element_type=F32)


def _split_bf16(x):
    hi = x.astype(BF16)
    lo = (x - hi.astype(F32)).astype(BF16)
    return hi, lo


def _in_proj_kernel(x_ref, g_ref, w_ref, p_ref, *, row_chunks):
    rows = x_ref.shape[0] // row_chunks
    for c in range(row_chunks):
        rs = slice(c * rows, (c + 1) * rows)
        xn = _rmsnorm(x_ref[rs, :], g_ref[...]).astype(BF16)
        p_ref[rs, :] = _dot(xn, w_ref[...])


def _in_proj(x, g, w, *, tm=512, row_chunks=1):
    m, d = x.shape
    n = w.shape[1]
    vmem = _nbytes((d, n), BF16) + 2 * _nbytes((tm, d), F32) + 2 * _nbytes((tm, n), F32)
    return pl.pallas_call(
        functools.partial(_in_proj_kernel, row_chunks=row_chunks),
        grid=(m // tm,),
        in_specs=[
            pl.BlockSpec((tm, d), lambda i: (i, 0)),
            pl.BlockSpec((1, d), lambda i: (0, 0)),
            pl.BlockSpec((d, n), lambda i: (0, 0), pipeline_mode=pl.Buffered(1)),
        ],
        out_specs=pl.BlockSpec((tm, n), lambda i: (i, 0)),
        out_shape=jax.ShapeDtypeStruct((m, n), F32),
        compiler_params=_params(("parallel",), vmem),
        name="in_proj",
    )(x, g, w)


def _sgu_kernel(u_ref, v_ref, w_ref, b_ref, g_ref, o_ref):
    rows = u_ref.shape[0]
    v = _rmsnorm(_gelu_tanh(v_ref[...]), g_ref[...]).astype(BF16)
    r = lax.broadcasted_iota(jnp.int32, (CHUNK, CHUNK), 0)
    c = lax.broadcasted_iota(jnp.int32, (CHUNK, CHUNK), 1)
    causal = c <= r
    for g in range(N_GROUPS):
        cs = slice(g * LANES, (g + 1) * LANES)
        w = jnp.where(causal, w_ref[g], 0.0).astype(BF16)
        bias = b_ref[g]
        for n in range(rows // CHUNK):
            rs = slice(n * CHUNK, (n + 1) * CHUNK)
            mix = _dot(w, v[rs, cs]) + bias
            o_ref[rs, cs] = (_gelu_tanh(u_ref[rs, cs]) * mix).astype(BF16)


def _sgu(p, w, b, g, *, width, ts=512):
    m = p.shape[0]
    vmem = 4 * _nbytes((ts, width), F32) + 2 * _nbytes((ts, width), BF16)
    return pl.pallas_call(
        _sgu_kernel,
        grid=(m // ts,),
        in_specs=[
            pl.BlockSpec((ts, width), lambda i: (i, 0)),
            pl.BlockSpec((ts, width), lambda i: (i, 1)),
            pl.BlockSpec(w.shape, lambda i: (0, 0, 0)),
            pl.BlockSpec(b.shape, lambda i: (0, 0, 0)),
            pl.BlockSpec((1, width), lambda i: (0, 0)),
        ],
        out_specs=pl.BlockSpec((ts, width), lambda i: (i, 0)),
        out_shape=jax.ShapeDtypeStruct((m, width), BF16),
        compiler_params=_params(("parallel",), vmem),
        name="sgu",
    )(p, p, w, b, g)


def _sconv_kernel(bg_ref, cg_ref, x_ref, w_ref, o_ref):
    z = cg_ref[...] * x_ref[...]
    o_ref[...] = (bg_ref[...] * _causal_conv(z, w_ref, SCONV_W)).astype(BF16)


def _sconv(p, w, *, batch, seq, col0):
    m = p.shape[0]
    width = w.shape[1]
    ng = width // LANES
    blk = (seq, LANES)
    vmem = 6 * _nbytes(blk, F32) + 2 * _nbytes(blk, BF16)
    return pl.pallas_call(
        _sconv_kernel,
        grid=(batch, ng),
        in_specs=[
            pl.BlockSpec(blk, lambda b, g: (b, col0 + g)),
            pl.BlockSpec(blk, lambda b, g: (b, col0 + ng + g)),
            pl.BlockSpec(blk, lambda b, g: (b, col0 + 2 * ng + g)),
            pl.BlockSpec((SCONV_W, LANES), lambda b, g: (0, g)),
        ],
        out_specs=pl.BlockSpec(blk, lambda b, g: (b, g)),
        out_shape=jax.ShapeDtypeStruct((m, width), BF16),
        compiler_params=_params(("parallel", "parallel"), vmem),
        name="sconv",
    )(p, p, p, w)


def _rope(x, cos, sin_signed):
    return x * cos + pltpu.roll(x, x.shape[1] // 2, axis=1) * sin_signed


def _moba_kernel(q_ref, k_ref, v_ref, cos_ref, sin_ref, o_ref,
                 kr_ref, va_ref, km_ref, s_ref):
    seq, dh = q_ref.shape
    blk = MOBA_BLOCK
    nblk = seq // blk
    sub = 8
    assert nblk <= sub and blk == 2 * dh
    q_scale = dh ** -0.5 * _LOG2_E

    km_ref[...] = jnp.zeros(km_ref.shape, F32)
    for n in range(nblk):
        rs = slice(n * blk, (n + 1) * blk)
        kr = _rope(k_ref[rs, :], cos_ref[rs, :], sin_ref[rs, :])
        kr_ref[rs, :] = kr.astype(BF16)
        km_ref[n:n + 1, :] = jnp.sum(kr, axis=0, keepdims=True) * (1.0 / blk)
        va_ref[rs, :dh] = v_ref[rs, :].astype(BF16)
        va_ref[rs, dh:] = jnp.ones((blk, dh), BF16)
    km_hi, km_lo = _split_bf16(km_ref[...])
    km_hh = jnp.concatenate([km_hi, km_hi], axis=1)

    row = lax.broadcasted_iota(jnp.int32, (blk, blk), 0)
    col = lax.broadcasted_iota(jnp.int32, (blk, blk), 1)
    causal = col <= row
    blk_id = lax.broadcasted_iota(jnp.int32, (sub, blk), 0)
    er = lax.broadcasted_iota(jnp.int32, (LANES, nblk * dh), 0)
    ec = lax.broadcasted_iota(jnp.int32, (LANES, nblk * dh), 1)
    expand = jnp.where(er == ec // dh, 1.0, 0.0).astype(BF16)

    for i in range(nblk):
        rs = slice(i * blk, (i + 1) * blk)
        qr = _rope(q_ref[rs, :], cos_ref[rs, :], sin_ref[rs, :])
        qs = (qr * q_scale).astype(BF16)

        choose = i > MOBA_TOPK
        if choose:
            q_hi, q_lo = _split_bf16(qr)
            q_hl = jnp.concatenate([q_hi, q_lo], axis=1)
            gate = (_dot_nt(km_hh, q_hl) + _dot_nt(km_lo, q_hi))[:sub, :]
            cnt = jnp.zeros(gate.shape, F32)
            for m in range(i):
                gm = gate[m:m + 1, :]
                beats = (gm > gate) | ((gm == gate) & (blk_id > m))
                cnt = cnt + jnp.where(beats, 1.0, 0.0)
            sel_t = jnp.where(cnt < float(MOBA_TOPK), 1.0, 0.0)
            sel_t = jnp.concatenate([sel_t, jnp.zeros((LANES - sub, blk), F32)], axis=0)
            sel = sel_t.T.astype(BF16)
            keep = [None] * i
            for n in range(0, i, 2):
                u = _dot(sel, expand[:, n * dh:(n + 2) * dh])
                keep[n] = jnp.concatenate([u[:, :dh], u[:, :dh]], axis=1) > 0.5
                if n + 1 < i:
                    keep[n + 1] = jnp.concatenate([u[:, dh:], u[:, dh:]], axis=1) > 0.5

        m_tile = None
        for j in range(i + 1):
            ks = slice(j * blk, (j + 1) * blk)
            s = _dot_nt(qs, kr_ref[ks, :])
            if j == i:
                s = jnp.where(causal, s, NEG)
            elif choose:
                s = jnp.where(keep[j], s, NEG)
            s_ref[:, ks] = s
            mj = jnp.maximum(s[:, :dh], s[:, dh:])
            m_tile = mj if m_tile is None else jnp.maximum(m_tile, mj)
        m_row = jnp.max(m_tile, axis=-1, keepdims=True)

        acc = jnp.zeros((blk, 2 * dh), F32)
        for j in range(i + 1):
            ks = slice(j * blk, (j + 1) * blk)
            p = jnp.exp2(s_ref[:, ks] - m_row)
            acc = acc + _dot(p.astype(BF16), va_ref[ks, :])
        o_ref[rs, :] = (acc[:, :dh] / acc[:, dh:]).astype(BF16)


def _moba(p, cos, sin_signed, *, batch, seq, heads, col0):
    m = p.shape[0]
    dh = LANES
    blk = (seq, dh)
    vmem = (6 * _nbytes(blk, F32) + 4 * _nbytes(blk, F32) + 2 * _nbytes(blk, BF16)
            + 3 * _nbytes(blk, BF16) + _nbytes((LANES, dh), F32)
            + _nbytes((MOBA_BLOCK, seq), F32))
    return pl.pallas_call(
        _moba_kernel,
        grid=(batch, heads),
        in_specs=[
            pl.BlockSpec(blk, lambda b, h: (b, col0 + h)),
            pl.BlockSpec(blk, lambda b, h: (b, col0 + heads + h)),
            pl.BlockSpec(blk, lambda b, h: (b, col0 + 2 * heads + h)),
            pl.BlockSpec(blk, lambda b, h: (0, 0)),
            pl.BlockSpec(blk, lambda b, h: (0, 0)),
        ],
        out_specs=pl.BlockSpec(blk, lambda b, h: (b, h)),
        out_shape=jax.ShapeDtypeStruct((m, heads * dh), BF16),
        scratch_shapes=[
            pltpu.VMEM(blk, BF16),
            pltpu.VMEM((seq, 2 * dh), BF16),
            pltpu.VMEM((LANES, dh), F32),
            pltpu.VMEM((MOBA_BLOCK, seq), F32),
        ],
        compiler_params=_params(("parallel", "parallel"), vmem),
        name="moba",
    )(p, p, p, cos, sin_signed)


def _lru_kernel(xr_ref, gate_ref, cw_ref, cb_ref, wa_ref, ba_ref, wx_ref, bx_ref,
                lam_ref, o_ref, a_ref, h_ref):
    seq = xr_ref.shape[0]
    seg_len = seq // SUBLANES
    seg_stride = a_ref.shape[0] // SUBLANES

    xc = _causal_conv(xr_ref[...], cw_ref, LRU_CONV_W) + cb_ref[...]
    xb = xc.astype(BF16)
    r = _sigmoid(_dot(xb, wa_ref[0]) + ba_ref[...])
    i = _sigmoid(_dot(xb, wx_ref[0]) + bx_ref[...])
    log_a = (-LRU_C * r) * _softplus(-lam_ref[...])
    a = jnp.exp(log_a)
    one_minus_a2 = jnp.tanh(-log_a) * (1.0 + a * a)
    b = jnp.sqrt(one_minus_a2) * (i * xc)

    for s in range(SUBLANES):
        a_ref[s * seg_stride:s * seg_stride + seg_len, :] = a[s * seg_len:(s + 1) * seg_len]
        h_ref[s * seg_stride:s * seg_stride + seg_len, :] = b[s * seg_len:(s + 1) * seg_len]
    h = jnp.zeros((SUBLANES, LANES), F32)
    prod = jnp.ones((SUBLANES, LANES), F32)
    for t in range(seg_len):
        rows = pl.ds(t, SUBLANES, stride=seg_stride)
        at = a_ref[rows, :]
        h = at * h + h_ref[rows, :]
        prod = at * prod
        h_ref[rows, :] = h
        a_ref[rows, :] = prod
    gate = _gelu_tanh(gate_ref[...])
    carry = jnp.zeros((1, LANES), F32)
    for s in range(SUBLANES):
        seg = slice(s * seg_stride, s * seg_stride + seg_len)
        out = slice(s * seg_len, (s + 1) * seg_len)
        hs = h_ref[seg, :] + a_ref[seg, :] * carry
        o_ref[out, :] = (gate[out, :] * hs).astype(BF16)
        carry = h[s:s + 1, :] + prod[s:s + 1, :] * carry


def _lru(p, cw, cb, wa, ba, wx, bx, lam, *, batch, seq, col0):
    m = p.shape[0]
    width = cw.shape[1]
    ng = width // LANES
    blk = (seq, LANES)
    vec = pl.BlockSpec((1, LANES), lambda b, g: (0, g))
    mat = pl.BlockSpec((1, LANES, LANES), lambda b, g: (g, 0, 0))
    seg_stride = seq // SUBLANES + SUBLANES
    scan_buf = (SUBLANES * seg_stride, LANES)
    vmem = (4 * _nbytes(blk, F32) + 2 * _nbytes(blk, BF16) + 12 * _nbytes(blk, F32)
            + 2 * _nbytes(scan_buf, F32))
    return pl.pallas_call(
        _lru_kernel,
        grid=(batch, ng),
        in_specs=[
            pl.BlockSpec(blk, lambda b, g: (b, col0 + g)),
            pl.BlockSpec(blk, lambda b, g: (b, col0 + ng + g)),
            pl.BlockSpec((LRU_CONV_W, LANES), lambda b, g: (0, g)),
            vec, mat, vec, mat, vec, vec,
        ],
        out_specs=pl.BlockSpec(blk, lambda b, g: (b, g)),
        out_shape=jax.ShapeDtypeStruct((m, width), BF16),
        scratch_shapes=[pltpu.VMEM(scan_buf, F32), pltpu.VMEM(scan_buf, F32)],
        compiler_params=_params(("parallel", "parallel"), vmem),
        name="rg_lru",
    )(p, p, cw, cb, wa, ba, wx, bx, lam)


def _merge_kernel(x_ref, g_ref, oa_ref, ob_ref, oc_ref, od_ref, wg_ref, bg_ref, wb_ref,
                  wo_ref, out_ref, xn_ref, acc_ref):
    j = pl.program_id(1)

    @pl.when(j == 0)
    def _():
        xn_ref[...] = _rmsnorm(x_ref[...], g_ref[...]).astype(BF16)
        acc_ref[...] = jnp.zeros(acc_ref.shape, F32)

    xn = xn_ref[...]
    y = None
    for k, o_ref in enumerate((oa_ref, ob_ref, oc_ref, od_ref)):
        gate = _sigmoid(_dot(xn, wg_ref[k]) + bg_ref[k])
        t = gate * _dot(o_ref[...], wb_ref[k])
        y = t if y is None else y + t
    acc_ref[...] += _dot(y.astype(BF16), wo_ref[...])

    @pl.when(j == pl.num_programs(1) - 1)
    def _():
        out_ref[...] = x_ref[...] + acc_ref[...]


def _merge(x, g, outs, wg, bg, wb, wo, *, tm=512, tn=256):
    m, d = x.shape
    bw = outs[0].shape[1]
    nb = len(outs)
    vmem = (4 * _nbytes((tm, d), F32) + 2 * nb * _nbytes((tm, bw), BF16)
            + 2 * _nbytes((nb, d, tn), BF16) + 2 * _nbytes((nb, bw, tn), BF16)
            + 2 * _nbytes((tn, d), BF16) + _nbytes((tm, d), BF16) + _nbytes((tm, d), F32))
    o_spec = pl.BlockSpec((tm, bw), lambda i, j: (i, 0))
    return pl.pallas_call(
        _merge_kernel,
        grid=(m // tm, d // tn),
        in_specs=[
            pl.BlockSpec((tm, d), lambda i, j: (i, 0)),
            pl.BlockSpec((1, d), lambda i, j: (0, 0)),
            o_spec, o_spec, o_spec, o_spec,
            pl.BlockSpec((nb, d, tn), lambda i, j: (0, 0, j)),
            pl.BlockSpec((nb, 1, tn), lambda i, j: (0, 0, j)),
            pl.BlockSpec((nb, bw, tn), lambda i, j: (0, 0, j)),
            pl.BlockSpec((tn, d), lambda i, j: (j, 0)),
        ],
        out_specs=pl.BlockSpec((tm, d), lambda i, j: (i, 0)),
        out_shape=jax.ShapeDtypeStruct((m, d), F32),
        scratch_shapes=[pltpu.VMEM((tm, d), BF16), pltpu.VMEM((tm, d), F32)],
        compiler_params=_params(("parallel", "arbitrary"), vmem),
        name="merge",
    )(x, g, *outs, wg, bg, wb, wo)


def _ffn_kernel(*refs, final):
    if final:
        x_ref, g_ref, w1_ref, w3_ref, w2_ref, gf_ref, out_ref, xn_ref, acc_ref = refs
    else:
        x_ref, g_ref, w1_ref, w3_ref, w2_ref, out_ref, xn_ref, acc_ref = refs
    j = pl.program_id(1)

    @pl.when(j == 0)
    def _():
        xn_ref[...] = _rmsnorm(x_ref[...], g_ref[...]).astype(BF16)
        acc_ref[...] = jnp.zeros(acc_ref.shape, F32)

    xn = xn_ref[...]
    h1 = _dot(xn, w1_ref[...])
    h = (h1 * _sigmoid(h1)) * _dot(xn, w3_ref[...])
    acc_ref[...] += _dot(h.astype(BF16), w2_ref[...])

    @pl.when(j == pl.num_programs(1) - 1)
    def _():
        y = x_ref[...] + acc_ref[...]
        out_ref[...] = _rmsnorm(y, gf_ref[...]) if final else y


def _ffn(x, g, w1, w3, w2, g_final=None, *, tm=512, tf=512):
    m, d = x.shape
    f = w1.shape[1]
    final = g_final is not None
    vmem = (4 * _nbytes((tm, d), F32) + 4 * _nbytes((d, tf), BF16) + 2 * _nbytes((tf, d), BF16)
            + _nbytes((tm, d), BF16) + _nbytes((tm, d), F32))
    row = pl.BlockSpec((1, d), lambda i, j: (0, 0))
    in_specs = [
        pl.BlockSpec((tm, d), lambda i, j: (i, 0)),
        row,
        pl.BlockSpec((d, tf), lambda i, j: (0, j)),
        pl.BlockSpec((d, tf), lambda i, j: (0, j)),
        pl.BlockSpec((tf, d), lambda i, j: (j, 0)),
    ]
    args = [x, g, w1, w3, w2]
    if final:
        in_specs.append(row)
        args.append(g_final)
    return pl.pallas_call(
        functools.partial(_ffn_kernel, final=final),
        grid=(m // tm, f // tf),
        in_specs=in_specs,
        out_specs=pl.BlockSpec((tm, d), lambda i, j: (i, 0)),
        out_shape=jax.ShapeDtypeStruct((m, d), F32),
        scratch_shapes=[pltpu.VMEM((tm, d), BF16), pltpu.VMEM((tm, d), F32)],
        compiler_params=_params(("parallel", "arbitrary"), vmem),
        name="ffn_final" if final else "ffn",
    )(*args)


def _rope_tables(seq, dh):
    half = dh // 2
    inv = ROPE_THETA ** (-jnp.arange(half, dtype=F32) / half)
    ang = jnp.arange(seq).astype(F32)[:, None] * inv[None, :]
    cos, sin = jnp.cos(ang), jnp.sin(ang)
    return jnp.concatenate([cos, cos], axis=-1), jnp.concatenate([-sin, sin], axis=-1)


def kernel(x, g_mix, w_in, w_sgu, b_sgu, g_sgu, w_sconv, w_lru_conv, b_lru_conv, w_lru_a,
           b_lru_a, w_lru_x, b_lru_x, lru_lambda, w_gate, b_gate, w_branch, w_out, g_ffn,
           w_ffn1, w_ffn3, w_ffn2, g_final):
    batch, seq, d = x.shape
    depth = w_in.shape[0]
    bw = w_branch.shape[2]
    heads = bw // LANES
    assert bw == N_GROUPS * LANES and seq % MOBA_BLOCK == 0 and w_sgu.shape[-1] == CHUNK
    col_b = 2 * bw // LANES
    col_c = 5 * bw // LANES
    col_d = 8 * bw // LANES

    cos, sin_signed = _rope_tables(seq, LANES)
    row = lambda v: v.reshape(1, -1)

    h = x.reshape(batch * seq, d)
    for l in range(depth):
        p = _in_proj(h, row(g_mix[l]), w_in[l].astype(BF16))
        o_a = _sgu(p, w_sgu[l], b_sgu[l].T[:, :, None], row(g_sgu[l]), width=bw)
        o_b = _sconv(p, w_sconv[l], batch=batch, seq=seq, col0=col_b)
        o_c = _moba(p, cos, sin_signed, batch=batch, seq=seq, heads=heads, col0=col_c)
        o_d = _lru(p, w_lru_conv[l], row(b_lru_conv[l]), w_lru_a[l].astype(BF16),
                   row(b_lru_a[l]), w_lru_x[l].astype(BF16), row(b_lru_x[l]),
                   row(lru_lambda[l]), batch=batch, seq=seq, col0=col_d)
        h = _merge(h, row(g_mix[l]), (o_a, o_b, o_c, o_d), w_gate[l].astype(BF16),
                   b_gate[l][:, None, :], w_branch[l].astype(BF16), w_out[l].astype(BF16))
        h = _ffn(h, row(g_ffn[l]), w_ffn1[l].astype(BF16), w_ffn3[l].astype(BF16),
                 w_ffn2[l].astype(BF16), row(g_final) if l == depth - 1 else None)
    return h.reshape(batch, seq, d)
```

```python
import functools
import math

import jax
import jax.numpy as jnp
import numpy as np
from jax import lax
from jax.experimental import pallas as pl
from jax.experimental.pallas import tpu as pltpu

N_BRANCH = 4
N_GROUPS = 4
CHUNK = 128
SCONV_W = 3
MOBA_BLOCK = 256
MOBA_TOPK = 3
ROPE_THETA = 10000.0
NEG = -1e30
LRU_CONV_W = 4
LRU_C = 8.0
EPS = 1e-6

LANES = 128
SUBLANES = 8
MXU_COLS = 256
V7X_VMEM_BYTES = 64 * 1024 * 1024
VMEM_BUDGET_BYTES = 56 * 1024 * 1024

F32 = jnp.float32
BF16 = jnp.bfloat16

_SQRT_2_OVER_PI = float(np.sqrt(2.0 / np.pi).astype(np.float32))
_LOG2_E = 1.4426950408889634


def _params(semantics, vmem_bytes):
    assert vmem_bytes <= VMEM_BUDGET_BYTES, vmem_bytes
    return pltpu.CompilerParams(dimension_semantics=semantics,
                                vmem_limit_bytes=VMEM_BUDGET_BYTES)


def _nbytes(shape, dtype):
    return int(np.prod(shape)) * jnp.dtype(dtype).itemsize


def _rmsnorm(x, g):
    var = jnp.mean(x * x, axis=-1, keepdims=True)
    return x * lax.rsqrt(var + EPS) * g


def _gelu_tanh(x):
    return x * (0.5 * (1.0 + jnp.tanh(_SQRT_2_OVER_PI * (x + 0.044715 * (x * x * x)))))


def _sigmoid(x):
    return 0.5 * (jnp.tanh(0.5 * x) + 1.0)


def _softplus(x):
    return jnp.maximum(x, 0.0) + jnp.log1p(jnp.exp(-jnp.abs(x)))


def _shift_rows(x, d):
    row = lax.broadcasted_iota(jnp.int32, x.shape, 0)
    return jnp.where(row >= d, pltpu.roll(x, d, axis=0), 0.0)


def _causal_conv(x, w_ref, width):
    acc = None
    for k in range(width):
        d = width - 1 - k
        term = (_shift_rows(x, d) if d else x) * w_ref[k:k + 1, :]
        acc = term if acc is None else acc + term
    return acc


def _dot(a, b):
    return jnp.dot(a, b, preferred_element_type=F32)


def _dot_nt(a, b):
    return lax.dot_general(a, b, (((1,), (1,)), ((), ())), preferred_element_type=F32)


def _split_bf16(x):
    hi = x.astype(BF16)
    lo = (x - hi.astype(F32)).astype(BF16)
    return hi, lo


def _norm_kernel(x_ref, g_ref, xn_ref):
    xn_ref[...] = _rmsnorm(x_ref[...], g_ref[...]).astype(BF16)


def _norm(x, g, *, tm=512):
    m, d = x.shape
    vmem = 2 * _nbytes((tm, d), F32) + 2 * _nbytes((tm, d), BF16)
    return pl.pallas_call(
        _norm_kernel,
        grid=(m // tm,),
        in_specs=[pl.BlockSpec((tm, d), lambda i: (i, 0)), pl.BlockSpec((1, d), lambda i: (0, 0))],
        out_specs=pl.BlockSpec((tm, d), lambda i: (i, 0)),
        out_shape=jax.ShapeDtypeStruct((m, d), BF16),
        compiler_params=_params(("parallel",), vmem),
        name="norm",
    )(x, g)


def _resident(shape, index_map):
    return pl.BlockSpec(shape, index_map, pipeline_mode=pl.Buffered(1))


def _in_proj_kernel(xn_ref, w_ref, p_ref):
    p_ref[...] = _dot(xn_ref[...], w_ref[...])


def _in_proj(xn, w, l, *, tm=512):
    m, d = xn.shape
    n = w.shape[2]
    vmem = _nbytes((d, n), BF16) + 2 * _nbytes((tm, d), BF16) + 2 * _nbytes((tm, n), F32)
    return pl.pallas_call(
        _in_proj_kernel,
        grid=(m // tm,),
        in_specs=[
            pl.BlockSpec((tm, d), lambda i: (i, 0)),
            _resident((None, d, n), lambda i: (l, 0, 0)),
        ],
        out_specs=pl.BlockSpec((tm, n), lambda i: (i, 0)),
        out_shape=jax.ShapeDtypeStruct((m, n), F32),
        compiler_params=_params(("parallel",), vmem),
        name="in_proj",
    )(xn, w)


def _sgu_kernel(u_ref, v_ref, w_ref, b_ref, g_ref, o_ref):
    rows = u_ref.shape[0]
    v = _rmsnorm(_gelu_tanh(v_ref[...]), g_ref[...]).astype(BF16)
    r = lax.broadcasted_iota(jnp.int32, (CHUNK, CHUNK), 0)
    c = lax.broadcasted_iota(jnp.int32, (CHUNK, CHUNK), 1)
    causal = c <= r
    for g in range(N_GROUPS):
        cs = slice(g * LANES, (g + 1) * LANES)
        w = jnp.where(causal, w_ref[g], 0.0).astype(BF16)
        bias = b_ref[g]
        for n in range(rows // CHUNK):
            rs = slice(n * CHUNK, (n + 1) * CHUNK)
            mix = _dot(w, v[rs, cs]) + bias
            o_ref[rs, cs] = (_gelu_tanh(u_ref[rs, cs]) * mix).astype(BF16)


def _sgu(p, w, b, g, *, width, ts=512):
    m = p.shape[0]
    vmem = 4 * _nbytes((ts, width), F32) + 2 * _nbytes((ts, width), BF16)
    return pl.pallas_call(
        _sgu_kernel,
        grid=(m // ts,),
        in_specs=[
            pl.BlockSpec((ts, width), lambda i: (i, 0)),
            pl.BlockSpec((ts, width), lambda i: (i, 1)),
            pl.BlockSpec(w.shape, lambda i: (0, 0, 0)),
            pl.BlockSpec(b.shape, lambda i: (0, 0, 0)),
            pl.BlockSpec((1, width), lambda i: (0, 0)),
        ],
        out_specs=pl.BlockSpec((ts, width), lambda i: (i, 0)),
        out_shape=jax.ShapeDtypeStruct((m, width), BF16),
        compiler_params=_params(("parallel",), vmem),
        name="sgu",
    )(p, p, w, b, g)


def _sconv_kernel(bg_ref, cg_ref, x_ref, w_ref, o_ref):
    z = cg_ref[...] * x_ref[...]
    o_ref[...] = (bg_ref[...] * _causal_conv(z, w_ref, SCONV_W)).astype(BF16)


def _sconv(p, w, *, batch, seq, col0):
    m = p.shape[0]
    width = w.shape[1]
    ng = width // LANES
    blk = (seq, LANES)
    vmem = 6 * _nbytes(blk, F32) + 2 * _nbytes(blk, BF16)
    return pl.pallas_call(
        _sconv_kernel,
        grid=(batch, ng),
        in_specs=[
            pl.BlockSpec(blk, lambda b, g: (b, col0 + g)),
            pl.BlockSpec(blk, lambda b, g: (b, col0 + ng + g)),
            pl.BlockSpec(blk, lambda b, g: (b, col0 + 2 * ng + g)),
            pl.BlockSpec((SCONV_W, LANES), lambda b, g: (0, g)),
        ],
        out_specs=pl.BlockSpec(blk, lambda b, g: (b, g)),
        out_shape=jax.ShapeDtypeStruct((m, width), BF16),
        compiler_params=_params(("parallel", "parallel"), vmem),
        name="sconv",
    )(p, p, p, w)


def _rope(x, cos, sin_signed):
    return x * cos + pltpu.roll(x, x.shape[1] // 2, axis=1) * sin_signed


def _moba_kernel(q_ref, k_ref, v_ref, cos_ref, sin_ref, o_ref,
                 kr_ref, va_ref, km_ref, s_ref):
    seq, dh = q_ref.shape
    blk = MOBA_BLOCK
    nblk = seq // blk
    sub = 8
    assert nblk <= sub and blk == 2 * dh
    q_scale = dh ** -0.5 * _LOG2_E

    km_ref[...] = jnp.zeros(km_ref.shape, F32)
    for n in range(nblk):
        rs = slice(n * blk, (n + 1) * blk)
        kr = _rope(k_ref[rs, :], cos_ref[rs, :], sin_ref[rs, :])
        kr_ref[rs, :] = kr.astype(BF16)
        km_ref[n:n + 1, :] = jnp.sum(kr, axis=0, keepdims=True) * (1.0 / blk)
        va_ref[rs, :dh] = v_ref[rs, :].astype(BF16)
        va_ref[rs, dh:] = jnp.ones((blk, dh), BF16)
    km_hi, km_lo = _split_bf16(km_ref[...])
    km_hh = jnp.concatenate([km_hi, km_hi], axis=1)

    row = lax.broadcasted_iota(jnp.int32, (blk, blk), 0)
    col = lax.broadcasted_iota(jnp.int32, (blk, blk), 1)
    causal = col <= row
    blk_id = lax.broadcasted_iota(jnp.int32, (sub, blk), 0)
    er = lax.broadcasted_iota(jnp.int32, (LANES, nblk * dh), 0)
    ec = lax.broadcasted_iota(jnp.int32, (LANES, nblk * dh), 1)
    expand = jnp.where(er == ec // dh, 1.0, 0.0).astype(BF16)

    for i in range(nblk):
        rs = slice(i * blk, (i + 1) * blk)
        qr = _rope(q_ref[rs, :], cos_ref[rs, :], sin_ref[rs, :])
        qs = (qr * q_scale).astype(BF16)

        choose = i > MOBA_TOPK
        if choose:
            q_hi, q_lo = _split_bf16(qr)
            q_hl = jnp.concatenate([q_hi, q_lo], axis=1)
            gate = (_dot_nt(km_hh, q_hl) + _dot_nt(km_lo, q_hi))[:sub, :]
            cnt = jnp.zeros(gate.shape, F32)
            for m in range(i):
                gm = gate[m:m + 1, :]
                beats = (gm > gate) | ((gm == gate) & (blk_id > m))
                cnt = cnt + jnp.where(beats, 1.0, 0.0)
            sel_t = jnp.where(cnt < float(MOBA_TOPK), 1.0, 0.0)
            sel_t = jnp.concatenate([sel_t, jnp.zeros((LANES - sub, blk), F32)], axis=0)
            sel = sel_t.T.astype(BF16)
            keep = [None] * i
            for n in range(0, i, 2):
                u = _dot(sel, expand[:, n * dh:(n + 2) * dh])
                keep[n] = jnp.concatenate([u[:, :dh], u[:, :dh]], axis=1) > 0.5
                if n + 1 < i:
                    keep[n + 1] = jnp.concatenate([u[:, dh:], u[:, dh:]], axis=1) > 0.5

        m_tile = None
        for j in range(i + 1):
            ks = slice(j * blk, (j + 1) * blk)
            s = _dot_nt(qs, kr_ref[ks, :])
            if j == i:
                s = jnp.where(causal, s, NEG)
            elif choose:
                s = jnp.where(keep[j], s, NEG)
            s_ref[:, ks] = s
            mj = jnp.maximum(s[:, :dh], s[:, dh:])
            m_tile = mj if m_tile is None else jnp.maximum(m_tile, mj)
        m_row = jnp.max(m_tile, axis=-1, keepdims=True)

        acc = jnp.zeros((blk, 2 * dh), F32)
        for j in range(i + 1):
            ks = slice(j * blk, (j + 1) * blk)
            p = jnp.exp2(s_ref[:, ks] - m_row)
            acc = acc + _dot(p.astype(BF16), va_ref[ks, :])
        o_ref[rs, :] = (acc[:, :dh] / acc[:, dh:]).astype(BF16)


def _moba(p, cos, sin_signed, *, batch, seq, heads, col0):
    m = p.shape[0]
    dh = LANES
    blk = (seq, dh)
    vmem = (6 * _nbytes(blk, F32) + 4 * _nbytes(blk, F32) + 2 * _nbytes(blk, BF16)
            + 3 * _nbytes(blk, BF16) + _nbytes((LANES, dh), F32)
            + _nbytes((MOBA_BLOCK, seq), F32))
    return pl.pallas_call(
        _moba_kernel,
        grid=(batch, heads),
        in_specs=[
            pl.BlockSpec(blk, lambda b, h: (b, col0 + h)),
            pl.BlockSpec(blk, lambda b, h: (b, col0 + heads + h)),
            pl.BlockSpec(blk, lambda b, h: (b, col0 + 2 * heads + h)),
            pl.BlockSpec(blk, lambda b, h: (0, 0)),
            pl.BlockSpec(blk, lambda b, h: (0, 0)),
        ],
        out_specs=pl.BlockSpec(blk, lambda b, h: (b, h)),
        out_shape=jax.ShapeDtypeStruct((m, heads * dh), BF16),
        scratch_shapes=[
            pltpu.VMEM(blk, BF16),
            pltpu.VMEM((seq, 2 * dh), BF16),
            pltpu.VMEM((LANES, dh), F32),
            pltpu.VMEM((MOBA_BLOCK, seq), F32),
        ],
        compiler_params=_params(("parallel", "parallel"), vmem),
        name="moba",
    )(p, p, p, cos, sin_signed)


def _lru_kernel(xr_ref, gate_ref, cw_ref, cb_ref, wa_ref, ba_ref, wx_ref, bx_ref,
                lam_ref, o_ref, a_ref, h_ref):
    seq = xr_ref.shape[0]
    seg_len = seq // SUBLANES
    seg_stride = a_ref.shape[0] // SUBLANES

    xc = _causal_conv(xr_ref[...], cw_ref, LRU_CONV_W) + cb_ref[...]
    xb = xc.astype(BF16)
    r = _sigmoid(_dot(xb, wa_ref[0]) + ba_ref[...])
    i = _sigmoid(_dot(xb, wx_ref[0]) + bx_ref[...])
    log_a = (-LRU_C * r) * _softplus(-lam_ref[...])
    a = jnp.exp(log_a)
    one_minus_a2 = jnp.tanh(-log_a) * (1.0 + a * a)
    b = jnp.sqrt(one_minus_a2) * (i * xc)

    for s in range(SUBLANES):
        a_ref[s * seg_stride:s * seg_stride + seg_len, :] = a[s * seg_len:(s + 1) * seg_len]
        h_ref[s * seg_stride:s * seg_stride + seg_len, :] = b[s * seg_len:(s + 1) * seg_len]
    h = jnp.zeros((SUBLANES, LANES), F32)
    prod = jnp.ones((SUBLANES, LANES), F32)
    for t in range(seg_len):
        rows = pl.ds(t, SUBLANES, stride=seg_stride)
        at = a_ref[rows, :]
        h = at * h + h_ref[rows, :]
        prod = at * prod
        h_ref[rows, :] = h
        a_ref[rows, :] = prod
    gate = _gelu_tanh(gate_ref[...])
    carry = jnp.zeros((1, LANES), F32)
    for s in range(SUBLANES):
        seg = slice(s * seg_stride, s * seg_stride + seg_len)
        out = slice(s * seg_len, (s + 1) * seg_len)
        hs = h_ref[seg, :] + a_ref[seg, :] * carry
        o_ref[out, :] = (gate[out, :] * hs).astype(BF16)
        carry = h[s:s + 1, :] + prod[s:s + 1, :] * carry


def _lru(p, cw, cb, wa, ba, wx, bx, lam, *, batch, seq, col0):
    m = p.shape[0]
    width = cw.shape[1]
    ng = width // LANES
    blk = (seq, LANES)
    vec = pl.BlockSpec((1, LANES), lambda b, g: (0, g))
    mat = pl.BlockSpec((1, LANES, LANES), lambda b, g: (g, 0, 0))
    seg_stride = seq // SUBLANES + SUBLANES
    scan_buf = (SUBLANES * seg_stride, LANES)
    vmem = (4 * _nbytes(blk, F32) + 2 * _nbytes(blk, BF16) + 12 * _nbytes(blk, F32)
            + 2 * _nbytes(scan_buf, F32))
    return pl.pallas_call(
        _lru_kernel,
        grid=(batch, ng),
        in_specs=[
            pl.BlockSpec(blk, lambda b, g: (b, col0 + g)),
            pl.BlockSpec(blk, lambda b, g: (b, col0 + ng + g)),
            pl.BlockSpec((LRU_CONV_W, LANES), lambda b, g: (0, g)),
            vec, mat, vec, mat, vec, vec,
        ],
        out_specs=pl.BlockSpec(blk, lambda b, g: (b, g)),
        out_shape=jax.ShapeDtypeStruct((m, width), BF16),
        scratch_shapes=[pltpu.VMEM(scan_buf, F32), pltpu.VMEM(scan_buf, F32)],
        compiler_params=_params(("parallel", "parallel"), vmem),
        name="rg_lru",
    )(p, p, cw, cb, wa, ba, wx, bx, lam)


def _gate_merge_kernel(xn_ref, oa_ref, ob_ref, oc_ref, od_ref, wg_ref, bg_ref, wb_ref, y_ref):
    xn = xn_ref[...]
    y = None
    for k, o_ref in enumerate((oa_ref, ob_ref, oc_ref, od_ref)):
        gate = _sigmoid(_dot(xn, wg_ref[k]) + bg_ref[k])
        t = gate * _dot(o_ref[...], wb_ref[k])
        y = t if y is None else y + t
    y_ref[...] = y.astype(BF16)


def _gate_merge(xn, outs, wg, bg, wb, l, *, tm=512, slabs=2):
    m, d = xn.shape
    bw = outs[0].shape[1]
    nb = len(outs)
    tn = d // slabs
    vmem = (_nbytes((nb, d, tn), BF16) + _nbytes((nb, bw, tn), BF16)
            + 2 * _nbytes((tm, d), BF16) + 2 * nb * _nbytes((tm, bw), BF16)
            + 2 * _nbytes((tm, tn), BF16) + 3 * _nbytes((tm, tn), F32))
    o_spec = pl.BlockSpec((tm, bw), lambda s, i: (i, 0))
    return pl.pallas_call(
        _gate_merge_kernel,
        grid=(slabs, m // tm),
        in_specs=[
            pl.BlockSpec((tm, d), lambda s, i: (i, 0)),
            o_spec, o_spec, o_spec, o_spec,
            _resident((None, nb, d, tn), lambda s, i: (l, 0, 0, s)),
            _resident((nb, 1, tn), lambda s, i: (0, 0, s)),
            _resident((None, nb, bw, tn), lambda s, i: (l, 0, 0, s)),
        ],
        out_specs=pl.BlockSpec((tm, tn), lambda s, i: (i, s)),
        out_shape=jax.ShapeDtypeStruct((m, d), BF16),
        compiler_params=_params(("arbitrary", "arbitrary"), vmem),
        name="gate_merge",
    )(xn, *outs, wg, bg, wb)


def _residual_proj_kernel(x_ref, a_ref, w_ref, g_ref, *out_refs, final):
    x_new = x_ref[...] + _dot(a_ref[...], w_ref[...])
    if final:
        out_refs[0][...] = _rmsnorm(x_new, g_ref[...])
    else:
        out_refs[0][...] = x_new
        out_refs[1][...] = _rmsnorm(x_new, g_ref[...]).astype(BF16)


def _residual_proj(x, a, w, l, g, *, final, tm, name):
    m, d = x.shape
    k = a.shape[1]
    tile = pl.BlockSpec((tm, d), lambda i: (i, 0))
    vmem = (_nbytes((k, d), BF16) + 2 * _nbytes((tm, k), BF16) + 5 * _nbytes((tm, d), F32)
            + 2 * _nbytes((tm, d), BF16))
    x_out = jax.ShapeDtypeStruct((m, d), F32)
    return pl.pallas_call(
        functools.partial(_residual_proj_kernel, final=final),
        grid=(m // tm,),
        in_specs=[
            tile,
            pl.BlockSpec((tm, k), lambda i: (i, 0)),
            _resident((None, k, d), lambda i: (l, 0, 0)),
            pl.BlockSpec((1, d), lambda i: (0, 0)),
        ],
        out_specs=tile if final else (tile, tile),
        out_shape=x_out if final else (x_out, jax.ShapeDtypeStruct((m, d), BF16)),
        compiler_params=_params(("parallel",), vmem),
        name=name,
    )(x, a, w, g)


def _ffn_up_kernel(xn_ref, w1_ref, w3_ref, h_ref, *, tc):
    xn = xn_ref[...]
    for c in range(0, h_ref.shape[1], tc):
        h1 = _dot(xn, w1_ref[:, c:c + tc])
        h3 = _dot(xn, w3_ref[:, c:c + tc])
        h_ref[:, c:c + tc] = ((h1 * _sigmoid(h1)) * h3).astype(BF16)


def _ffn_up(xn, w1, w3, l, *, tm=512, slabs=2, tc=MXU_COLS):
    m, d = xn.shape
    f = w1.shape[2]
    tf = f // slabs
    assert tf % tc == 0
    vmem = (2 * _nbytes((d, tf), BF16) + 2 * _nbytes((tm, d), BF16)
            + 2 * _nbytes((tm, tf), BF16) + 6 * _nbytes((tm, tc), F32))
    w_spec = _resident((None, d, tf), lambda s, i: (l, 0, s))
    return pl.pallas_call(
        functools.partial(_ffn_up_kernel, tc=tc),
        grid=(slabs, m // tm),
        in_specs=[pl.BlockSpec((tm, d), lambda s, i: (i, 0)), w_spec, w_spec],
        out_specs=pl.BlockSpec((tm, tf), lambda s, i: (i, s)),
        out_shape=jax.ShapeDtypeStruct((m, f), BF16),
        compiler_params=_params(("arbitrary", "arbitrary"), vmem),
        name="ffn_up",
    )(xn, w1, w3)


def _rope_tables(seq, dh):
    half = dh // 2
    inv = ROPE_THETA ** (-jnp.arange(half, dtype=F32) / half)
    ang = jnp.arange(seq).astype(F32)[:, None] * inv[None, :]
    cos, sin = jnp.cos(ang), jnp.sin(ang)
    return jnp.concatenate([cos, cos], axis=-1), jnp.concatenate([-sin, sin], axis=-1)


def kernel(x, g_mix, w_in, w_sgu, b_sgu, g_sgu, w_sconv, w_lru_conv, b_lru_conv, w_lru_a,
           b_lru_a, w_lru_x, b_lru_x, lru_lambda, w_gate, b_gate, w_branch, w_out, g_ffn,
           w_ffn1, w_ffn3, w_ffn2, g_final):
    batch, seq, d = x.shape
    depth = w_in.shape[0]
    bw = w_branch.shape[2]
    heads = bw // LANES
    assert bw == N_GROUPS * LANES and seq % MOBA_BLOCK == 0 and w_sgu.shape[-1] == CHUNK
    col_b = 2 * bw // LANES
    col_c = 5 * bw // LANES
    col_d = 8 * bw // LANES

    cos, sin_signed = _rope_tables(seq, LANES)
    row = lambda v: v.reshape(1, -1)

    w_in, w_gate, w_branch, w_out, w_ffn1, w_ffn3, w_ffn2 = (
        w.astype(BF16) for w in (w_in, w_gate, w_branch, w_out, w_ffn1, w_ffn3, w_ffn2))

    h = x.reshape(batch * seq, d)
    hn = _norm(h, row(g_mix[0]))
    for l in range(depth):
        last = l == depth - 1
        p = _in_proj(hn, w_in, l)
        o_a = _sgu(p, w_sgu[l], b_sgu[l].T[:, :, None], row(g_sgu[l]), width=bw)
        o_b = _sconv(p, w_sconv[l], batch=batch, seq=seq, col0=col_b)
        o_c = _moba(p, cos, sin_signed, batch=batch, seq=seq, heads=heads, col0=col_c)
        o_d = _lru(p, w_lru_conv[l], row(b_lru_conv[l]), w_lru_a[l].astype(BF16),
                   row(b_lru_a[l]), w_lru_x[l].astype(BF16), row(b_lru_x[l]),
                   row(lru_lambda[l]), batch=batch, seq=seq, col0=col_d)
        y = _gate_merge(hn, (o_a, o_b, o_c, o_d), w_gate, b_gate[l][:, None, :], w_branch, l)
        h, hn = _residual_proj(h, y, w_out, l, row(g_ffn[l]), final=False,
                               tm=512, name="out_proj")
        hid = _ffn_up(hn, w_ffn1, w_ffn3, l)
        if last:
            h = _residual_proj(h, hid, w_ffn2, l, row(g_final), final=True,
                               tm=256, name="ffn_down_final")
        else:
            h, hn = _residual_proj(h, hid, w_ffn2, l, row(g_mix[l + 1]),
                                   final=False, tm=256, name="ffn_down")
    return h.reshape(batch, seq, d)
```

```python
import functools
import math

import jax
import jax.numpy as jnp
import numpy as np
from jax import lax
from jax.experimental import pallas as pl
from jax.experimental.pallas import tpu as pltpu

N_BRANCH = 4
N_GROUPS = 4
CHUNK = 128
SCONV_W = 3
MOBA_BLOCK = 256
MOBA_TOPK = 3
ROPE_THETA = 10000.0
NEG = -1e30
LRU_CONV_W = 4
LRU_C = 8.0
EPS = 1e-6

LANES = 128
SUBLANES = 8
MXU_COLS = 256
V7X_VMEM_BYTES = 64 * 1024 * 1024
VMEM_BUDGET_BYTES = 56 * 1024 * 1024

F32 = jnp.float32
BF16 = jnp.bfloat16

_SQRT_2_OVER_PI = float(np.sqrt(2.0 / np.pi).astype(np.float32))
_LOG2_E = 1.4426950408889634


def _params(semantics, vmem_bytes):
    assert vmem_bytes <= VMEM_BUDGET_BYTES, vmem_bytes
    return pltpu.CompilerParams(dimension_semantics=semantics,
                                vmem_limit_bytes=VMEM_BUDGET_BYTES)


def _nbytes(shape, dtype):
    return int(np.prod(shape)) * jnp.dtype(dtype).itemsize


def _rmsnorm(x, g):
    var = jnp.mean(x * x, axis=-1, keepdims=True)
    return x * lax.rsqrt(var + EPS) * g


def _gelu_tanh(x):
    return x * (0.5 * (1.0 + jnp.tanh(_SQRT_2_OVER_PI * (x + 0.044715 * (x * x * x)))))


def _sigmoid(x):
    return 0.5 * (jnp.tanh(0.5 * x) + 1.0)


def _softplus(x):
    return jnp.maximum(x, 0.0) + jnp.log1p(jnp.exp(-jnp.abs(x)))


def _shift_rows(x, d, halo=None):
    row = lax.broadcasted_iota(jnp.int32, x.shape, 0)
    fill = 0.0
    if halo is not None:
        pad = jnp.zeros((x.shape[0] - SUBLANES, x.shape[1]), x.dtype)
        fill = jnp.concatenate([pltpu.roll(halo, d, axis=0), pad], axis=0)
    return jnp.where(row >= d, pltpu.roll(x, d, axis=0), fill)


def _causal_conv(x, w_ref, width, halo=None):
    acc = None
    for k in range(width):
        d = width - 1 - k
        term = (_shift_rows(x, d, halo) if d else x) * w_ref[k:k + 1, :]
        acc = term if acc is None else acc + term
    return acc


def _dot(a, b):
    return jnp.dot(a, b, preferred_element_type=F32)


def _dot_nt(a, b):
    return lax.dot_general(a, b, (((1,), (1,)), ((), ())), preferred_element_type=F32)


def _split_bf16(x):
    hi = x.astype(BF16)
    lo = (x - hi.astype(F32)).astype(BF16)
    return hi, lo


def _norm_kernel(x_ref, g_ref, xn_ref):
    xn_ref[...] = _rmsnorm(x_ref[...], g_ref[...]).astype(BF16)


def _norm(x, g, *, tm=512):
    m, d = x.shape
    vmem = 2 * _nbytes((tm, d), F32) + 2 * _nbytes((tm, d), BF16)
    return pl.pallas_call(
        _norm_kernel,
        grid=(m // tm,),
        in_specs=[pl.BlockSpec((tm, d), lambda i: (i, 0)), pl.BlockSpec((1, d), lambda i: (0, 0))],
        out_specs=pl.BlockSpec((tm, d), lambda i: (i, 0)),
        out_shape=jax.ShapeDtypeStruct((m, d), BF16),
        compiler_params=_params(("parallel",), vmem),
        name="norm",
    )(x, g)


def _resident(shape, index_map):
    return pl.BlockSpec(shape, index_map, pipeline_mode=pl.Buffered(1))


def _in_proj_kernel(xn_ref, w_ref, p_ref):
    p_ref[...] = _dot(xn_ref[...], w_ref[...])


def _in_proj(xn, w, l, *, tm=512):
    m, d = xn.shape
    n = w.shape[2]
    vmem = _nbytes((d, n), BF16) + 2 * _nbytes((tm, d), BF16) + 2 * _nbytes((tm, n), F32)
    return pl.pallas_call(
        _in_proj_kernel,
        grid=(m // tm,),
        in_specs=[
            pl.BlockSpec((tm, d), lambda i: (i, 0)),
            _resident((None, d, n), lambda i: (l, 0, 0)),
        ],
        out_specs=pl.BlockSpec((tm, n), lambda i: (i, 0)),
        out_shape=jax.ShapeDtypeStruct((m, n), F32),
        compiler_params=_params(("parallel",), vmem),
        name="in_proj",
    )(xn, w)


def _in_proj_mix_kernel(xn_ref, w_ref, wsgu_ref, bsgu_ref, gsgu_ref, wsc_ref,
                        cw_ref, cb_ref, wa_ref, ba_ref, wx_ref, bx_ref, lam_ref,
                        qkv_ref, oa_ref, ob_ref, od_ref,
                        zprev_ref, xprev_ref, hcarry_ref, a_ref, h_ref, *, tiles_per_seq):
    tm, bw = oa_ref.shape

    @pl.when(pl.program_id(0) % tiles_per_seq == 0)
    def _():
        zprev_ref[...] = jnp.zeros(zprev_ref.shape, F32)
        xprev_ref[...] = jnp.zeros(xprev_ref.shape, F32)
        hcarry_ref[...] = jnp.zeros(hcarry_ref.shape, F32)

    xn = xn_ref[...]

    def proj(c):
        return _dot(xn, w_ref[:, c * bw:(c + 1) * bw])

    groups = [slice(g * LANES, (g + 1) * LANES) for g in range(N_GROUPS)]

    xr, gate_in = proj(8), proj(9)

    xhalo = xprev_ref[...]
    xprev_ref[...] = xr[tm - SUBLANES:, :]
    xc = _causal_conv(xr, cw_ref, LRU_CONV_W, xhalo) + cb_ref[...]
    xb = xc.astype(BF16)

    u, v = proj(0), proj(1)

    r_pre = jnp.concatenate([_dot(xb[:, cs], wa_ref[g]) for g, cs in enumerate(groups)], axis=1)
    i_pre = jnp.concatenate([_dot(xb[:, cs], wx_ref[g]) for g, cs in enumerate(groups)], axis=1)

    vn = _rmsnorm(_gelu_tanh(v), gsgu_ref[...]).astype(BF16)
    gu = _gelu_tanh(u)

    c_gate, x_conv, b_gate = proj(3), proj(4), proj(2)

    row = lax.broadcasted_iota(jnp.int32, (CHUNK, CHUNK), 0)
    col = lax.broadcasted_iota(jnp.int32, (CHUNK, CHUNK), 1)
    for g, cs in enumerate(groups):
        w = jnp.where(col <= row, wsgu_ref[g], 0.0).astype(BF16)
        bias = bsgu_ref[g]
        for n in range(tm // CHUNK):
            rs = slice(n * CHUNK, (n + 1) * CHUNK)
            oa_ref[rs, cs] = (gu[rs, cs] * (_dot(w, vn[rs, cs]) + bias)).astype(BF16)

    r = _sigmoid(r_pre + ba_ref[...])
    i = _sigmoid(i_pre + bx_ref[...])
    log_a = (-LRU_C * r) * _softplus(-lam_ref[...])
    a = jnp.exp(log_a)
    b = jnp.sqrt(jnp.tanh(-log_a) * (1.0 + a * a)) * (i * xc)

    qkv_ref[:, 0:bw] = proj(5)

    seg_len = tm // SUBLANES
    seg_stride = a_ref.shape[1] // SUBLANES
    for g, cs in enumerate(groups):
        for s in range(SUBLANES):
            dst = slice(s * seg_stride, s * seg_stride + seg_len)
            a_ref[g, dst, :] = a[s * seg_len:(s + 1) * seg_len, cs]
            h_ref[g, dst, :] = b[s * seg_len:(s + 1) * seg_len, cs]
    h = [jnp.zeros((SUBLANES, LANES), F32)] * N_GROUPS
    prod = [jnp.ones((SUBLANES, LANES), F32)] * N_GROUPS
    for t in range(seg_len):
        rows = pl.ds(t, SUBLANES, stride=seg_stride)
        for g in range(N_GROUPS):
            at = a_ref[g, rows, :]
            h[g] = at * h[g] + h_ref[g, rows, :]
            prod[g] = at * prod[g]
            h_ref[g, rows, :] = h[g]
            a_ref[g, rows, :] = prod[g]

    qkv_ref[:, bw:2 * bw] = proj(6)

    gate = _gelu_tanh(gate_in)
    for g, cs in enumerate(groups):
        carry = hcarry_ref[0:1, cs]
        for s in range(SUBLANES):
            seg = slice(s * seg_stride, s * seg_stride + seg_len)
            out = slice(s * seg_len, (s + 1) * seg_len)
            hs = h_ref[g, seg, :] + a_ref[g, seg, :] * carry
            od_ref[out, cs] = (gate[out, cs] * hs).astype(BF16)
            carry = h[g][s:s + 1, :] + prod[g][s:s + 1, :] * carry
        hcarry_ref[0:1, cs] = carry

    z = c_gate * x_conv
    zhalo = zprev_ref[...]
    zprev_ref[...] = z[tm - SUBLANES:, :]
    ob_ref[...] = (b_gate * _causal_conv(z, wsc_ref, SCONV_W, zhalo)).astype(BF16)

    qkv_ref[:, 2 * bw:3 * bw] = proj(7)


def _in_proj_mix(xn, w, l, sgu, sconv_w, lru, *, seq, bw, tm=512):
    m, d = xn.shape
    n = w.shape[2]
    assert seq % tm == 0 and n == 10 * bw
    seg_stride = tm // SUBLANES + SUBLANES
    scan_buf = (bw // LANES, SUBLANES * seg_stride, LANES)
    halo = (SUBLANES, bw)
    vmem = (_nbytes((d, n), BF16) + 2 * _nbytes((tm, d), BF16) + 2 * _nbytes((tm, 3 * bw), F32)
            + 6 * _nbytes((tm, bw), BF16) + 2 * _nbytes(scan_buf, F32) + 16 * _nbytes((tm, bw), F32))
    full = lambda a: pl.BlockSpec(a.shape, lambda i, nd=a.ndim: (0,) * nd)
    small = (*sgu, sconv_w, *lru)
    o_spec = pl.BlockSpec((tm, bw), lambda i: (i, 0))
    o_shape = jax.ShapeDtypeStruct((m, bw), BF16)
    return pl.pallas_call(
        functools.partial(_in_proj_mix_kernel, tiles_per_seq=seq // tm),
        grid=(m // tm,),
        in_specs=[
            pl.BlockSpec((tm, d), lambda i: (i, 0)),
            _resident((None, d, n), lambda i: (l, 0, 0)),
            *[full(a) for a in small],
        ],
        out_specs=(pl.BlockSpec((tm, 3 * bw), lambda i: (i, 0)), o_spec, o_spec, o_spec),
        out_shape=(jax.ShapeDtypeStruct((m, 3 * bw), F32), o_shape, o_shape, o_shape),
        scratch_shapes=[pltpu.VMEM(halo, F32), pltpu.VMEM(halo, F32), pltpu.VMEM(halo, F32),
                        pltpu.VMEM(scan_buf, F32), pltpu.VMEM(scan_buf, F32)],
        compiler_params=_params(("arbitrary",), vmem),
        name="in_proj_mix",
    )(xn, w, *small)


def _sgu_kernel(u_ref, v_ref, w_ref, b_ref, g_ref, o_ref):
    rows = u_ref.shape[0]
    v = _rmsnorm(_gelu_tanh(v_ref[...]), g_ref[...]).astype(BF16)
    r = lax.broadcasted_iota(jnp.int32, (CHUNK, CHUNK), 0)
    c = lax.broadcasted_iota(jnp.int32, (CHUNK, CHUNK), 1)
    causal = c <= r
    for g in range(N_GROUPS):
        cs = slice(g * LANES, (g + 1) * LANES)
        w = jnp.where(causal, w_ref[g], 0.0).astype(BF16)
        bias = b_ref[g]
        for n in range(rows // CHUNK):
            rs = slice(n * CHUNK, (n + 1) * CHUNK)
            mix = _dot(w, v[rs, cs]) + bias
            o_ref[rs, cs] = (_gelu_tanh(u_ref[rs, cs]) * mix).astype(BF16)


def _sgu(p, w, b, g, *, width, ts=512):
    m = p.shape[0]
    vmem = 4 * _nbytes((ts, width), F32) + 2 * _nbytes((ts, width), BF16)
    return pl.pallas_call(
        _sgu_kernel,
        grid=(m // ts,),
        in_specs=[
            pl.BlockSpec((ts, width), lambda i: (i, 0)),
            pl.BlockSpec((ts, width), lambda i: (i, 1)),
            pl.BlockSpec(w.shape, lambda i: (0, 0, 0)),
            pl.BlockSpec(b.shape, lambda i: (0, 0, 0)),
            pl.BlockSpec((1, width), lambda i: (0, 0)),
        ],
        out_specs=pl.BlockSpec((ts, width), lambda i: (i, 0)),
        out_shape=jax.ShapeDtypeStruct((m, width), BF16),
        compiler_params=_params(("parallel",), vmem),
        name="sgu",
    )(p, p, w, b, g)


def _sconv_kernel(bg_ref, cg_ref, x_ref, w_ref, o_ref):
    z = cg_ref[...] * x_ref[...]
    o_ref[...] = (bg_ref[...] * _causal_conv(z, w_ref, SCONV_W)).astype(BF16)


def _sconv(p, w, *, batch, seq, col0):
    m = p.shape[0]
    width = w.shape[1]
    ng = width // LANES
    blk = (seq, LANES)
    vmem = 6 * _nbytes(blk, F32) + 2 * _nbytes(blk, BF16)
    return pl.pallas_call(
        _sconv_kernel,
        grid=(batch, ng),
        in_specs=[
            pl.BlockSpec(blk, lambda b, g: (b, col0 + g)),
            pl.BlockSpec(blk, lambda b, g: (b, col0 + ng + g)),
            pl.BlockSpec(blk, lambda b, g: (b, col0 + 2 * ng + g)),
            pl.BlockSpec((SCONV_W, LANES), lambda b, g: (0, g)),
        ],
        out_specs=pl.BlockSpec(blk, lambda b, g: (b, g)),
        out_shape=jax.ShapeDtypeStruct((m, width), BF16),
        compiler_params=_params(("parallel", "parallel"), vmem),
        name="sconv",
    )(p, p, p, w)


def _rope(x, cos, sin_signed):
    return x * cos + pltpu.roll(x, x.shape[1] // 2, axis=1) * sin_signed


def _moba_kernel(q_ref, k_ref, v_ref, cos_ref, sin_ref, o_ref,
                 kr_ref, va_ref, km_ref, s_ref):
    seq, dh = q_ref.shape
    blk = MOBA_BLOCK
    nblk = seq // blk
    sub = 8
    assert nblk <= sub and blk == 2 * dh
    q_scale = dh ** -0.5 * _LOG2_E

    km_ref[...] = jnp.zeros(km_ref.shape, F32)
    for n in range(nblk):
        rs = slice(n * blk, (n + 1) * blk)
        kr = _rope(k_ref[rs, :], cos_ref[rs, :], sin_ref[rs, :])
        kr_ref[rs, :] = kr.astype(BF16)
        km_ref[n:n + 1, :] = jnp.sum(kr, axis=0, keepdims=True) * (1.0 / blk)
        va_ref[rs, :dh] = v_ref[rs, :].astype(BF16)
        va_ref[rs, dh:] = jnp.ones((blk, dh), BF16)
    km_hi, km_lo = _split_bf16(km_ref[...])
    km_hh = jnp.concatenate([km_hi, km_hi], axis=1)

    row = lax.broadcasted_iota(jnp.int32, (blk, blk), 0)
    col = lax.broadcasted_iota(jnp.int32, (blk, blk), 1)
    causal = col <= row
    blk_id = lax.broadcasted_iota(jnp.int32, (sub, blk), 0)
    er = lax.broadcasted_iota(jnp.int32, (LANES, nblk * dh), 0)
    ec = lax.broadcasted_iota(jnp.int32, (LANES, nblk * dh), 1)
    expand = jnp.where(er == ec // dh, 1.0, 0.0).astype(BF16)

    for i in range(nblk):
        rs = slice(i * blk, (i + 1) * blk)
        qr = _rope(q_ref[rs, :], cos_ref[rs, :], sin_ref[rs, :])
        qs = (qr * q_scale).astype(BF16)

        choose = i > MOBA_TOPK
        if choose:
            q_hi, q_lo = _split_bf16(qr)
            q_hl = jnp.concatenate([q_hi, q_lo], axis=1)
            gate = (_dot_nt(km_hh, q_hl) + _dot_nt(km_lo, q_hi))[:sub, :]
            cnt = jnp.zeros(gate.shape, F32)
            for m in range(i):
                gm = gate[m:m + 1, :]
                beats = (gm > gate) | ((gm == gate) & (blk_id > m))
                cnt = cnt + jnp.where(beats, 1.0, 0.0)
            sel_t = jnp.where(cnt < float(MOBA_TOPK), 1.0, 0.0)
            sel_t = jnp.concatenate([sel_t, jnp.zeros((LANES - sub, blk), F32)], axis=0)
            sel = sel_t.T.astype(BF16)
            keep = [None] * i
            for n in range(0, i, 2):
                u = _dot(sel, expand[:, n * dh:(n + 2) * dh])
                keep[n] = jnp.concatenate([u[:, :dh], u[:, :dh]], axis=1) > 0.5
                if n + 1 < i:
                    keep[n + 1] = jnp.concatenate([u[:, dh:], u[:, dh:]], axis=1) > 0.5

        m_tile = None
        for j in range(i + 1):
            ks = slice(j * blk, (j + 1) * blk)
            s = _dot_nt(qs, kr_ref[ks, :])
            if j == i:
                s = jnp.where(causal, s, NEG)
            elif choose:
                s = jnp.where(keep[j], s, NEG)
            s_ref[:, ks] = s
            mj = jnp.maximum(s[:, :dh], s[:, dh:])
            m_tile = mj if m_tile is None else jnp.maximum(m_tile, mj)
        m_row = jnp.max(m_tile, axis=-1, keepdims=True)

        acc = jnp.zeros((blk, 2 * dh), F32)
        for j in range(i + 1):
            ks = slice(j * blk, (j + 1) * blk)
            p = jnp.exp2(s_ref[:, ks] - m_row)
            acc = acc + _dot(p.astype(BF16), va_ref[ks, :])
        o_ref[rs, :] = (acc[:, :dh] / acc[:, dh:]).astype(BF16)


def _moba(p, cos, sin_signed, *, batch, seq, heads, col0):
    m = p.shape[0]
    dh = LANES
    blk = (seq, dh)
    vmem = (6 * _nbytes(blk, F32) + 4 * _nbytes(blk, F32) + 2 * _nbytes(blk, BF16)
            + 3 * _nbytes(blk, BF16) + _nbytes((LANES, dh), F32)
            + _nbytes((MOBA_BLOCK, seq), F32))
    return pl.pallas_call(
        _moba_kernel,
        grid=(batch, heads),
        in_specs=[
            pl.BlockSpec(blk, lambda b, h: (b, col0 + h)),
            pl.BlockSpec(blk, lambda b, h: (b, col0 + heads + h)),
            pl.BlockSpec(blk, lambda b, h: (b, col0 + 2 * heads + h)),
            pl.BlockSpec(blk, lambda b, h: (0, 0)),
            pl.BlockSpec(blk, lambda b, h: (0, 0)),
        ],
        out_specs=pl.BlockSpec(blk, lambda b, h: (b, h)),
        out_shape=jax.ShapeDtypeStruct((m, heads * dh), BF16),
        scratch_shapes=[
            pltpu.VMEM(blk, BF16),
            pltpu.VMEM((seq, 2 * dh), BF16),
            pltpu.VMEM((LANES, dh), F32),
            pltpu.VMEM((MOBA_BLOCK, seq), F32),
        ],
        compiler_params=_params(("parallel", "parallel"), vmem),
        name="moba",
    )(p, p, p, cos, sin_signed)


def _lru_kernel(xr_ref, gate_ref, cw_ref, cb_ref, wa_ref, ba_ref, wx_ref, bx_ref,
                lam_ref, o_ref, a_ref, h_ref):
    seq = xr_ref.shape[0]
    seg_len = seq // SUBLANES
    seg_stride = a_ref.shape[0] // SUBLANES

    xc = _causal_conv(xr_ref[...], cw_ref, LRU_CONV_W) + cb_ref[...]
    xb = xc.astype(BF16)
    r = _sigmoid(_dot(xb, wa_ref[0]) + ba_ref[...])
    i = _sigmoid(_dot(xb, wx_ref[0]) + bx_ref[...])
    log_a = (-LRU_C * r) * _softplus(-lam_ref[...])
    a = jnp.exp(log_a)
    one_minus_a2 = jnp.tanh(-log_a) * (1.0 + a * a)
    b = jnp.sqrt(one_minus_a2) * (i * xc)

    for s in range(SUBLANES):
        a_ref[s * seg_stride:s * seg_stride + seg_len, :] = a[s * seg_len:(s + 1) * seg_len]
        h_ref[s * seg_stride:s * seg_stride + seg_len, :] = b[s * seg_len:(s + 1) * seg_len]
    h = jnp.zeros((SUBLANES, LANES), F32)
    prod = jnp.ones((SUBLANES, LANES), F32)
    for t in range(seg_len):
        rows = pl.ds(t, SUBLANES, stride=seg_stride)
        at = a_ref[rows, :]
        h = at * h + h_ref[rows, :]
        prod = at * prod
        h_ref[rows, :] = h
        a_ref[rows, :] = prod
    gate = _gelu_tanh(gate_ref[...])
    carry = jnp.zeros((1, LANES), F32)
    for s in range(SUBLANES):
        seg = slice(s * seg_stride, s * seg_stride + seg_len)
        out = slice(s * seg_len, (s + 1) * seg_len)
        hs = h_ref[seg, :] + a_ref[seg, :] * carry
        o_ref[out, :] = (gate[out, :] * hs).astype(BF16)
        carry = h[s:s + 1, :] + prod[s:s + 1, :] * carry


def _lru(p, cw, cb, wa, ba, wx, bx, lam, *, batch, seq, col0):
    m = p.shape[0]
    width = cw.shape[1]
    ng = width // LANES
    blk = (seq, LANES)
    vec = pl.BlockSpec((1, LANES), lambda b, g: (0, g))
    mat = pl.BlockSpec((1, LANES, LANES), lambda b, g: (g, 0, 0))
    seg_stride = seq // SUBLANES + SUBLANES
    scan_buf = (SUBLANES * seg_stride, LANES)
    vmem = (4 * _nbytes(blk, F32) + 2 * _nbytes(blk, BF16) + 12 * _nbytes(blk, F32)
            + 2 * _nbytes(scan_buf, F32))
    return pl.pallas_call(
        _lru_kernel,
        grid=(batch, ng),
        in_specs=[
            pl.BlockSpec(blk, lambda b, g: (b, col0 + g)),
            pl.BlockSpec(blk, lambda b, g: (b, col0 + ng + g)),
            pl.BlockSpec((LRU_CONV_W, LANES), lambda b, g: (0, g)),
            vec, mat, vec, mat, vec, vec,
        ],
        out_specs=pl.BlockSpec(blk, lambda b, g: (b, g)),
        out_shape=jax.ShapeDtypeStruct((m, width), BF16),
        scratch_shapes=[pltpu.VMEM(scan_buf, F32), pltpu.VMEM(scan_buf, F32)],
        compiler_params=_params(("parallel", "parallel"), vmem),
        name="rg_lru",
    )(p, p, cw, cb, wa, ba, wx, bx, lam)


def _gate_merge_kernel(xn_ref, oa_ref, ob_ref, oc_ref, od_ref, wg_ref, bg_ref, wb_ref, y_ref):
    xn = xn_ref[...]
    y = None
    for k, o_ref in enumerate((oa_ref, ob_ref, oc_ref, od_ref)):
        gate = _sigmoid(_dot(xn, wg_ref[k]) + bg_ref[k])
        t = gate * _dot(o_ref[...], wb_ref[k])
        y = t if y is None else y + t
    y_ref[...] = y.astype(BF16)


def _gate_merge(xn, outs, wg, bg, wb, l, *, tm=512, slabs=2):
    m, d = xn.shape
    bw = outs[0].shape[1]
    nb = len(outs)
    tn = d // slabs
    vmem = (_nbytes((nb, d, tn), BF16) + _nbytes((nb, bw, tn), BF16)
            + 2 * _nbytes((tm, d), BF16) + 2 * nb * _nbytes((tm, bw), BF16)
            + 2 * _nbytes((tm, tn), BF16) + 3 * _nbytes((tm, tn), F32))
    o_spec = pl.BlockSpec((tm, bw), lambda s, i: (i, 0))
    return pl.pallas_call(
        _gate_merge_kernel,
        grid=(slabs, m // tm),
        in_specs=[
            pl.BlockSpec((tm, d), lambda s, i: (i, 0)),
            o_spec, o_spec, o_spec, o_spec,
            _resident((None, nb, d, tn), lambda s, i: (l, 0, 0, s)),
            _resident((nb, 1, tn), lambda s, i: (0, 0, s)),
            _resident((None, nb, bw, tn), lambda s, i: (l, 0, 0, s)),
        ],
        out_specs=pl.BlockSpec((tm, tn), lambda s, i: (i, s)),
        out_shape=jax.ShapeDtypeStruct((m, d), BF16),
        compiler_params=_params(("arbitrary", "arbitrary"), vmem),
        name="gate_merge",
    )(xn, *outs, wg, bg, wb)


def _residual_proj_kernel(x_ref, a_ref, w_ref, g_ref, *out_refs, final):
    x_new = x_ref[...] + _dot(a_ref[...], w_ref[...])
    if final:
        out_refs[0][...] = _rmsnorm(x_new, g_ref[...])
    else:
        out_refs[0][...] = x_new
        out_refs[1][...] = _rmsnorm(x_new, g_ref[...]).astype(BF16)


def _residual_proj(x, a, w, l, g, *, final, tm, name):
    m, d = x.shape
    k = a.shape[1]
    tile = pl.BlockSpec((tm, d), lambda i: (i, 0))
    vmem = (_nbytes((k, d), BF16) + 2 * _nbytes((tm, k), BF16) + 5 * _nbytes((tm, d), F32)
            + 2 * _nbytes((tm, d), BF16))
    x_out = jax.ShapeDtypeStruct((m, d), F32)
    return pl.pallas_call(
        functools.partial(_residual_proj_kernel, final=final),
        grid=(m // tm,),
        in_specs=[
            tile,
            pl.BlockSpec((tm, k), lambda i: (i, 0)),
            _resident((None, k, d), lambda i: (l, 0, 0)),
            pl.BlockSpec((1, d), lambda i: (0, 0)),
        ],
        out_specs=tile if final else (tile, tile),
        out_shape=x_out if final else (x_out, jax.ShapeDtypeStruct((m, d), BF16)),
        compiler_params=_params(("parallel",), vmem),
        name=name,
    )(x, a, w, g)


def _ffn_up_kernel(xn_ref, w1_ref, w3_ref, h_ref, *, tc):
    xn = xn_ref[...]
    for c in range(0, h_ref.shape[1], tc):
        h1 = _dot(xn, w1_ref[:, c:c + tc])
        h3 = _dot(xn, w3_ref[:, c:c + tc])
        h_ref[:, c:c + tc] = ((h1 * _sigmoid(h1)) * h3).astype(BF16)


def _ffn_up(xn, w1, w3, l, *, tm=512, slabs=2, tc=MXU_COLS):
    m, d = xn.shape
    f = w1.shape[2]
    tf = f // slabs
    assert tf % tc == 0
    vmem = (2 * _nbytes((d, tf), BF16) + 2 * _nbytes((tm, d), BF16)
            + 2 * _nbytes((tm, tf), BF16) + 6 * _nbytes((tm, tc), F32))
    w_spec = _resident((None, d, tf), lambda s, i: (l, 0, s))
    return pl.pallas_call(
        functools.partial(_ffn_up_kernel, tc=tc),
        grid=(slabs, m // tm),
        in_specs=[pl.BlockSpec((tm, d), lambda s, i: (i, 0)), w_spec, w_spec],
        out_specs=pl.BlockSpec((tm, tf), lambda s, i: (i, s)),
        out_shape=jax.ShapeDtypeStruct((m, f), BF16),
        compiler_params=_params(("arbitrary", "arbitrary"), vmem),
        name="ffn_up",
    )(xn, w1, w3)


def _rope_tables(seq, dh):
    half = dh // 2
    inv = ROPE_THETA ** (-jnp.arange(half, dtype=F32) / half)
    ang = jnp.arange(seq).astype(F32)[:, None] * inv[None, :]
    cos, sin = jnp.cos(ang), jnp.sin(ang)
    return jnp.concatenate([cos, cos], axis=-1), jnp.concatenate([-sin, sin], axis=-1)


def kernel(x, g_mix, w_in, w_sgu, b_sgu, g_sgu, w_sconv, w_lru_conv, b_lru_conv, w_lru_a,
           b_lru_a, w_lru_x, b_lru_x, lru_lambda, w_gate, b_gate, w_branch, w_out, g_ffn,
           w_ffn1, w_ffn3, w_ffn2, g_final):
    batch, seq, d = x.shape
    depth = w_in.shape[0]
    bw = w_branch.shape[2]
    heads = bw // LANES
    assert bw == N_GROUPS * LANES and seq % MOBA_BLOCK == 0 and w_sgu.shape[-1] == CHUNK
    col_b = 2 * bw // LANES
    col_c = 5 * bw // LANES
    col_d = 8 * bw // LANES

    cos, sin_signed = _rope_tables(seq, LANES)
    row = lambda v: v.reshape(1, -1)

    w_in, w_gate, w_branch, w_out, w_ffn1, w_ffn3, w_ffn2 = (
        w.astype(BF16) for w in (w_in, w_gate, w_branch, w_out, w_ffn1, w_ffn3, w_ffn2))

    h = x.reshape(batch * seq, d)
    hn = _norm(h, row(g_mix[0]))
    for l in range(depth):
        last = l == depth - 1
        qkv, o_a, o_b, o_d = _in_proj_mix(
            hn, w_in, l,
            (w_sgu[l], b_sgu[l].T[:, :, None], row(g_sgu[l])),
            w_sconv[l],
            (w_lru_conv[l], row(b_lru_conv[l]), w_lru_a[l].astype(BF16), row(b_lru_a[l]),
             w_lru_x[l].astype(BF16), row(b_lru_x[l]), row(lru_lambda[l])),
            seq=seq, bw=bw)
        o_c = _moba(qkv, cos, sin_signed, batch=batch, seq=seq, heads=heads, col0=0)
        y = _gate_merge(hn, (o_a, o_b, o_c, o_d), w_gate, b_gate[l][:, None, :], w_branch, l)
        h, hn = _residual_proj(h, y, w_out, l, row(g_ffn[l]), final=False,
                               tm=512, name="out_proj")
        hid = _ffn_up(hn, w_ffn1, w_ffn3, l)
        if last:
            h = _residual_proj(h, hid, w_ffn2, l, row(g_final), final=True,
                               tm=256, name="ffn_down_final")
        else:
            h, hn = _residual_proj(h, hid, w_ffn2, l, row(g_mix[l + 1]),
                                   final=False, tm=256, name="ffn_down")
    return h.reshape(batch, seq, d)
```

```python
import functools
import math

import jax
import jax.numpy as jnp
import numpy as np
from jax import lax
from jax.experimental import pallas as pl
from jax.experimental.pallas import tpu as pltpu

N_BRANCH = 4
N_GROUPS = 4
CHUNK = 128
SCONV_W = 3
MOBA_BLOCK = 256
MOBA_TOPK = 3
ROPE_THETA = 10000.0
NEG = -1e30
LRU_CONV_W = 4
LRU_C = 8.0
EPS = 1e-6

LANES = 128
SUBLANES = 8
MXU_COLS = 256
V7X_VMEM_BYTES = 64 * 1024 * 1024
VMEM_BUDGET_BYTES = 56 * 1024 * 1024

F32 = jnp.float32
BF16 = jnp.bfloat16

_SQRT_2_OVER_PI = float(np.sqrt(2.0 / np.pi).astype(np.float32))
_LOG2_E = 1.4426950408889634


def _params(semantics, vmem_bytes):
    assert vmem_bytes <= VMEM_BUDGET_BYTES, vmem_bytes
    return pltpu.CompilerParams(dimension_semantics=semantics,
                                vmem_limit_bytes=VMEM_BUDGET_BYTES)


def _nbytes(shape, dtype):
    return int(np.prod(shape)) * jnp.dtype(dtype).itemsize


def _rmsnorm(x, g):
    var = jnp.mean(x * x, axis=-1, keepdims=True)
    return x * lax.rsqrt(var + EPS) * g


def _gelu_tanh(x):
    return x * (0.5 * (1.0 + jnp.tanh(_SQRT_2_OVER_PI * (x + 0.044715 * (x * x * x)))))


def _sigmoid(x):
    return 0.5 * (jnp.tanh(0.5 * x) + 1.0)


def _softplus(x):
    return jnp.maximum(x, 0.0) + jnp.log1p(jnp.exp(-jnp.abs(x)))


def _shift_rows(x, d, halo=None):
    row = lax.broadcasted_iota(jnp.int32, x.shape, 0)
    fill = 0.0
    if halo is not None:
        pad = jnp.zeros((x.shape[0] - SUBLANES, x.shape[1]), x.dtype)
        fill = jnp.concatenate([pltpu.roll(halo, d, axis=0), pad], axis=0)
    return jnp.where(row >= d, pltpu.roll(x, d, axis=0), fill)


def _causal_conv(x, w_ref, width, halo=None):
    acc = None
    for k in range(width):
        d = width - 1 - k
        term = (_shift_rows(x, d, halo) if d else x) * w_ref[k:k + 1, :]
        acc = term if acc is None else acc + term
    return acc


def _dot(a, b):
    return jnp.dot(a, b, preferred_element_type=F32)


def _dot_nt(a, b):
    return lax.dot_general(a, b, (((1,), (1,)), ((), ())), preferred_element_type=F32)


def _split_bf16(x):
    hi = x.astype(BF16)
    lo = (x - hi.astype(F32)).astype(BF16)
    return hi, lo


def _norm_kernel(x_ref, g_ref, xn_ref):
    xn_ref[...] = _rmsnorm(x_ref[...], g_ref[...]).astype(BF16)


def _norm(x, g, *, tm=512):
    m, d = x.shape
    vmem = 2 * _nbytes((tm, d), F32) + 2 * _nbytes((tm, d), BF16)
    return pl.pallas_call(
        _norm_kernel,
        grid=(m // tm,),
        in_specs=[pl.BlockSpec((tm, d), lambda i: (i, 0)), pl.BlockSpec((1, d), lambda i: (0, 0))],
        out_specs=pl.BlockSpec((tm, d), lambda i: (i, 0)),
        out_shape=jax.ShapeDtypeStruct((m, d), BF16),
        compiler_params=_params(("parallel",), vmem),
        name="norm",
    )(x, g)


def _resident(shape, index_map):
    return pl.BlockSpec(shape, index_map, pipeline_mode=pl.Buffered(1))


def _in_proj_kernel(xn_ref, w_ref, p_ref):
    p_ref[...] = _dot(xn_ref[...], w_ref[...])


def _in_proj(xn, w, l, *, tm=512):
    m, d = xn.shape
    n = w.shape[2]
    vmem = _nbytes((d, n), BF16) + 2 * _nbytes((tm, d), BF16) + 2 * _nbytes((tm, n), F32)
    return pl.pallas_call(
        _in_proj_kernel,
        grid=(m // tm,),
        in_specs=[
            pl.BlockSpec((tm, d), lambda i: (i, 0)),
            _resident((None, d, n), lambda i: (l, 0, 0)),
        ],
        out_specs=pl.BlockSpec((tm, n), lambda i: (i, 0)),
        out_shape=jax.ShapeDtypeStruct((m, n), F32),
        compiler_params=_params(("parallel",), vmem),
        name="in_proj",
    )(xn, w)


def _in_proj_mix_kernel(xn_ref, w_ref, wsgu_ref, bsgu_ref, gsgu_ref, wsc_ref,
                        cw_ref, cb_ref, wa_ref, ba_ref, wx_ref, bx_ref, lam_ref,
                        qkv_ref, oa_ref, ob_ref, od_ref,
                        zprev_ref, xprev_ref, hcarry_ref, a_ref, h_ref, *, tiles_per_seq):
    tm, bw = oa_ref.shape

    @pl.when(pl.program_id(0) % tiles_per_seq == 0)
    def _():
        zprev_ref[...] = jnp.zeros(zprev_ref.shape, F32)
        xprev_ref[...] = jnp.zeros(xprev_ref.shape, F32)
        hcarry_ref[...] = jnp.zeros(hcarry_ref.shape, F32)

    xn = xn_ref[...]

    def proj(c):
        return _dot(xn, w_ref[:, c * bw:(c + 1) * bw])

    groups = [slice(g * LANES, (g + 1) * LANES) for g in range(N_GROUPS)]

    xr, gate_in = proj(8), proj(9)

    xhalo = xprev_ref[...]
    xprev_ref[...] = xr[tm - SUBLANES:, :]
    xc = _causal_conv(xr, cw_ref, LRU_CONV_W, xhalo) + cb_ref[...]
    xb = xc.astype(BF16)

    u, v = proj(0), proj(1)

    r_pre = jnp.concatenate([_dot(xb[:, cs], wa_ref[g]) for g, cs in enumerate(groups)], axis=1)
    i_pre = jnp.concatenate([_dot(xb[:, cs], wx_ref[g]) for g, cs in enumerate(groups)], axis=1)

    vn = _rmsnorm(_gelu_tanh(v), gsgu_ref[...]).astype(BF16)
    gu = _gelu_tanh(u)

    for c0 in range(0, bw, MXU_COLS):
        cs = slice(c0, c0 + MXU_COLS)
        c_gate, x_conv, b_gate = (_dot(xn, w_ref[:, c * bw + c0:c * bw + c0 + MXU_COLS])
                                  for c in (3, 4, 2))
        z = c_gate * x_conv
        zhalo = zprev_ref[:, cs]
        zprev_ref[:, cs] = z[tm - SUBLANES:, :]
        ob_ref[:, cs] = (b_gate * _causal_conv(z, wsc_ref.at[:, cs], SCONV_W, zhalo)).astype(BF16)

    row = lax.broadcasted_iota(jnp.int32, (CHUNK, CHUNK), 0)
    col = lax.broadcasted_iota(jnp.int32, (CHUNK, CHUNK), 1)
    for g, cs in enumerate(groups):
        w = jnp.where(col <= row, wsgu_ref[g], 0.0).astype(BF16)
        bias = bsgu_ref[g]
        for n in range(tm // CHUNK):
            rs = slice(n * CHUNK, (n + 1) * CHUNK)
            oa_ref[rs, cs] = (gu[rs, cs] * (_dot(w, vn[rs, cs]) + bias)).astype(BF16)

    r = _sigmoid(r_pre + ba_ref[...])
    i = _sigmoid(i_pre + bx_ref[...])
    log_a = (-LRU_C * r) * _softplus(-lam_ref[...])
    a = jnp.exp(log_a)
    b = jnp.sqrt(jnp.tanh(-log_a) * (1.0 + a * a)) * (i * xc)

    qkv_ref[:, 0:bw] = proj(5)

    seg_len = tm // SUBLANES
    seg_stride = a_ref.shape[1] // SUBLANES
    for g, cs in enumerate(groups):
        for s in range(SUBLANES):
            dst = slice(s * seg_stride, s * seg_stride + seg_len)
            a_ref[g, dst, :] = a[s * seg_len:(s + 1) * seg_len, cs]
            h_ref[g, dst, :] = b[s * seg_len:(s + 1) * seg_len, cs]
    h = [jnp.zeros((SUBLANES, LANES), F32)] * N_GROUPS
    prod = [jnp.ones((SUBLANES, LANES), F32)] * N_GROUPS
    for t in range(seg_len):
        rows = pl.ds(t, SUBLANES, stride=seg_stride)
        for g in range(N_GROUPS):
            at = a_ref[g, rows, :]
            h[g] = at * h[g] + h_ref[g, rows, :]
            prod[g] = at * prod[g]
            h_ref[g, rows, :] = h[g]
            a_ref[g, rows, :] = prod[g]

    qkv_ref[:, bw:2 * bw] = proj(6)

    gate = _gelu_tanh(gate_in)
    for g, cs in enumerate(groups):
        carry = hcarry_ref[0:1, cs]
        for s in range(SUBLANES):
            seg = slice(s * seg_stride, s * seg_stride + seg_len)
            out = slice(s * seg_len, (s + 1) * seg_len)
            hs = h_ref[g, seg, :] + a_ref[g, seg, :] * carry
            od_ref[out, cs] = (gate[out, cs] * hs).astype(BF16)
            carry = h[g][s:s + 1, :] + prod[g][s:s + 1, :] * carry
        hcarry_ref[0:1, cs] = carry

    qkv_ref[:, 2 * bw:3 * bw] = proj(7)


def _in_proj_mix(xn, w, l, sgu, sconv_w, lru, *, seq, bw, tm=512):
    m, d = xn.shape
    n = w.shape[2]
    assert seq % tm == 0 and n == 10 * bw
    seg_stride = tm // SUBLANES + SUBLANES
    scan_buf = (bw // LANES, SUBLANES * seg_stride, LANES)
    halo = (SUBLANES, bw)
    vmem = (_nbytes((d, n), BF16) + 2 * _nbytes((tm, d), BF16) + 2 * _nbytes((tm, 3 * bw), F32)
            + 6 * _nbytes((tm, bw), BF16) + 2 * _nbytes(scan_buf, F32) + 16 * _nbytes((tm, bw), F32))
    full = lambda a: pl.BlockSpec(a.shape, lambda i, nd=a.ndim: (0,) * nd)
    small = (*sgu, sconv_w, *lru)
    o_spec = pl.BlockSpec((tm, bw), lambda i: (i, 0))
    o_shape = jax.ShapeDtypeStruct((m, bw), BF16)
    return pl.pallas_call(
        functools.partial(_in_proj_mix_kernel, tiles_per_seq=seq // tm),
        grid=(m // tm,),
        in_specs=[
            pl.BlockSpec((tm, d), lambda i: (i, 0)),
            _resident((None, d, n), lambda i: (l, 0, 0)),
            *[full(a) for a in small],
        ],
        out_specs=(pl.BlockSpec((tm, 3 * bw), lambda i: (i, 0)), o_spec, o_spec, o_spec),
        out_shape=(jax.ShapeDtypeStruct((m, 3 * bw), F32), o_shape, o_shape, o_shape),
        scratch_shapes=[pltpu.VMEM(halo, F32), pltpu.VMEM(halo, F32), pltpu.VMEM(halo, F32),
                        pltpu.VMEM(scan_buf, F32), pltpu.VMEM(scan_buf, F32)],
        compiler_params=_params(("arbitrary",), vmem),
        name="in_proj_mix",
    )(xn, w, *small)


def _sgu_kernel(u_ref, v_ref, w_ref, b_ref, g_ref, o_ref):
    rows = u_ref.shape[0]
    v = _rmsnorm(_gelu_tanh(v_ref[...]), g_ref[...]).astype(BF16)
    r = lax.broadcasted_iota(jnp.int32, (CHUNK, CHUNK), 0)
    c = lax.broadcasted_iota(jnp.int32, (CHUNK, CHUNK), 1)
    causal = c <= r
    for g in range(N_GROUPS):
        cs = slice(g * LANES, (g + 1) * LANES)
        w = jnp.where(causal, w_ref[g], 0.0).astype(BF16)
        bias = b_ref[g]
        for n in range(rows // CHUNK):
            rs = slice(n * CHUNK, (n + 1) * CHUNK)
            mix = _dot(w, v[rs, cs]) + bias
            o_ref[rs, cs] = (_gelu_tanh(u_ref[rs, cs]) * mix).astype(BF16)


def _sgu(p, w, b, g, *, width, ts=512):
    m = p.shape[0]
    vmem = 4 * _nbytes((ts, width), F32) + 2 * _nbytes((ts, width), BF16)
    return pl.pallas_call(
        _sgu_kernel,
        grid=(m // ts,),
        in_specs=[
            pl.BlockSpec((ts, width), lambda i: (i, 0)),
            pl.BlockSpec((ts, width), lambda i: (i, 1)),
            pl.BlockSpec(w.shape, lambda i: (0, 0, 0)),
            pl.BlockSpec(b.shape, lambda i: (0, 0, 0)),
            pl.BlockSpec((1, width), lambda i: (0, 0)),
        ],
        out_specs=pl.BlockSpec((ts, width), lambda i: (i, 0)),
        out_shape=jax.ShapeDtypeStruct((m, width), BF16),
        compiler_params=_params(("parallel",), vmem),
        name="sgu",
    )(p, p, w, b, g)


def _sconv_kernel(bg_ref, cg_ref, x_ref, w_ref, o_ref):
    z = cg_ref[...] * x_ref[...]
    o_ref[...] = (bg_ref[...] * _causal_conv(z, w_ref, SCONV_W)).astype(BF16)


def _sconv(p, w, *, batch, seq, col0):
    m = p.shape[0]
    width = w.shape[1]
    ng = width // LANES
    blk = (seq, LANES)
    vmem = 6 * _nbytes(blk, F32) + 2 * _nbytes(blk, BF16)
    return pl.pallas_call(
        _sconv_kernel,
        grid=(batch, ng),
        in_specs=[
            pl.BlockSpec(blk, lambda b, g: (b, col0 + g)),
            pl.BlockSpec(blk, lambda b, g: (b, col0 + ng + g)),
            pl.BlockSpec(blk, lambda b, g: (b, col0 + 2 * ng + g)),
            pl.BlockSpec((SCONV_W, LANES), lambda b, g: (0, g)),
        ],
        out_specs=pl.BlockSpec(blk, lambda b, g: (b, g)),
        out_shape=jax.ShapeDtypeStruct((m, width), BF16),
        compiler_params=_params(("parallel", "parallel"), vmem),
        name="sconv",
    )(p, p, p, w)


def _rope(x, cos, sin_signed):
    return x * cos + pltpu.roll(x, x.shape[1] // 2, axis=1) * sin_signed


def _moba_kernel(q_ref, k_ref, v_ref, cos_ref, sin_ref, o_ref,
                 kr_ref, va_ref, km_ref, s_ref):
    seq, dh = q_ref.shape
    blk = MOBA_BLOCK
    nblk = seq // blk
    sub = 8
    assert nblk <= sub and blk == 2 * dh
    q_scale = dh ** -0.5 * _LOG2_E

    km_ref[...] = jnp.zeros(km_ref.shape, F32)
    for n in range(nblk):
        rs = slice(n * blk, (n + 1) * blk)
        kr = _rope(k_ref[rs, :], cos_ref[rs, :], sin_ref[rs, :])
        kr_ref[rs, :] = kr.astype(BF16)
        km_ref[n:n + 1, :] = jnp.sum(kr, axis=0, keepdims=True) * (1.0 / blk)
        va_ref[rs, :dh] = v_ref[rs, :].astype(BF16)
        va_ref[rs, dh:] = jnp.ones((blk, dh), BF16)
    km_hi, km_lo = _split_bf16(km_ref[...])
    km_hh = jnp.concatenate([km_hi, km_hi], axis=1)

    row = lax.broadcasted_iota(jnp.int32, (blk, blk), 0)
    col = lax.broadcasted_iota(jnp.int32, (blk, blk), 1)
    causal = col <= row
    blk_id = lax.broadcasted_iota(jnp.int32, (sub, blk), 0)
    er = lax.broadcasted_iota(jnp.int32, (LANES, nblk * dh), 0)
    ec = lax.broadcasted_iota(jnp.int32, (LANES, nblk * dh), 1)
    expand = jnp.where(er == ec // dh, 1.0, 0.0).astype(BF16)

    def rotary_queries(i):
        rs = slice(i * blk, (i + 1) * blk)
        qr = _rope(q_ref[rs, :], cos_ref[rs, :], sin_ref[rs, :])
        qs = (qr * q_scale).astype(BF16)
        if i <= MOBA_TOPK:
            return qs, None
        q_hi, q_lo = _split_bf16(qr)
        q_hl = jnp.concatenate([q_hi, q_lo], axis=1)
        return qs, (_dot_nt(km_hh, q_hl) + _dot_nt(km_lo, q_hi))[:sub, :]

    def row_masks(i, gate):
        if gate is None:
            return None
        cnt = jnp.zeros(gate.shape, F32)
        for m in range(i):
            gm = gate[m:m + 1, :]
            beats = (gm > gate) | ((gm == gate) & (blk_id > m))
            cnt = cnt + jnp.where(beats, 1.0, 0.0)
        sel_t = jnp.where(cnt < float(MOBA_TOPK), 1.0, 0.0)
        sel_t = jnp.concatenate([sel_t, jnp.zeros((LANES - sub, blk), F32)], axis=0)
        sel = sel_t.T.astype(BF16)
        keep = [None] * i
        for n in range(0, i, 2):
            u = _dot(sel, expand[:, n * dh:(n + 2) * dh])
            keep[n] = jnp.concatenate([u[:, :dh], u[:, :dh]], axis=1) > 0.5
            if n + 1 < i:
                keep[n + 1] = jnp.concatenate([u[:, dh:], u[:, dh:]], axis=1) > 0.5
        return keep

    def scores(slot, i, j, qs, keep):
        ks = slice(j * blk, (j + 1) * blk)
        s = _dot_nt(qs, kr_ref[ks, :])
        if j == i:
            s = jnp.where(causal, s, NEG)
        elif keep is not None:
            s = jnp.where(keep[j], s, NEG)
        s_ref[slot, :, ks] = s
        return jnp.maximum(s[:, :dh], s[:, dh:])

    def weighted(slot, j, m_row):
        ks = slice(j * blk, (j + 1) * blk)
        p = jnp.exp2(s_ref[slot, :, ks] - m_row)
        return _dot(p.astype(BF16), va_ref[ks, :])

    def all_scores(i, qs, keep):
        m_tile = None
        for j in range(i + 1):
            mj = scores(i % 2, i, j, qs, keep)
            m_tile = mj if m_tile is None else jnp.maximum(m_tile, mj)
        return jnp.max(m_tile, axis=-1, keepdims=True)

    def queries(i):
        return rotary_queries(i) if i < nblk else (None, None)

    qs0, gate0 = queries(0)
    m_row = all_scores(0, qs0, row_masks(0, gate0))
    qs1, gate1 = queries(1)
    keep1 = row_masks(1, gate1)
    for i in range(nblk):
        qs2, gate2 = queries(i + 2)
        m_next = all_scores(i + 1, qs1, keep1) if i + 1 < nblk else None
        keep2 = row_masks(i + 2, gate2) if i + 2 < nblk else None
        acc = jnp.zeros((blk, 2 * dh), F32)
        for j in range(i + 1):
            acc = acc + weighted(i % 2, j, m_row)
        o_ref[i * blk:(i + 1) * blk, :] = (acc[:, :dh] / acc[:, dh:]).astype(BF16)
        m_row, qs1, keep1 = m_next, qs2, keep2


def _moba(p, cos, sin_signed, *, batch, seq, heads, col0):
    m = p.shape[0]
    dh = LANES
    blk = (seq, dh)
    vmem = (6 * _nbytes(blk, F32) + 4 * _nbytes(blk, F32) + 2 * _nbytes(blk, BF16)
            + 3 * _nbytes(blk, BF16) + _nbytes((LANES, dh), F32)
            + 2 * _nbytes((MOBA_BLOCK, seq), F32))
    return pl.pallas_call(
        _moba_kernel,
        grid=(batch, heads),
        in_specs=[
            pl.BlockSpec(blk, lambda b, h: (b, col0 + h)),
            pl.BlockSpec(blk, lambda b, h: (b, col0 + heads + h)),
            pl.BlockSpec(blk, lambda b, h: (b, col0 + 2 * heads + h)),
            pl.BlockSpec(blk, lambda b, h: (0, 0)),
            pl.BlockSpec(blk, lambda b, h: (0, 0)),
        ],
        out_specs=pl.BlockSpec(blk, lambda b, h: (b, h)),
        out_shape=jax.ShapeDtypeStruct((m, heads * dh), BF16),
        scratch_shapes=[
            pltpu.VMEM(blk, BF16),
            pltpu.VMEM((seq, 2 * dh), BF16),
            pltpu.VMEM((LANES, dh), F32),
            pltpu.VMEM((2, MOBA_BLOCK, seq), F32),
        ],
        compiler_params=_params(("parallel", "parallel"), vmem),
        name="moba",
    )(p, p, p, cos, sin_signed)


def _lru_kernel(xr_ref, gate_ref, cw_ref, cb_ref, wa_ref, ba_ref, wx_ref, bx_ref,
                lam_ref, o_ref, a_ref, h_ref):
    seq = xr_ref.shape[0]
    seg_len = seq // SUBLANES
    seg_stride = a_ref.shape[0] // SUBLANES

    xc = _causal_conv(xr_ref[...], cw_ref, LRU_CONV_W) + cb_ref[...]
    xb = xc.astype(BF16)
    r = _sigmoid(_dot(xb, wa_ref[0]) + ba_ref[...])
    i = _sigmoid(_dot(xb, wx_ref[0]) + bx_ref[...])
    log_a = (-LRU_C * r) * _softplus(-lam_ref[...])
    a = jnp.exp(log_a)
    one_minus_a2 = jnp.tanh(-log_a) * (1.0 + a * a)
    b = jnp.sqrt(one_minus_a2) * (i * xc)

    for s in range(SUBLANES):
        a_ref[s * seg_stride:s * seg_stride + seg_len, :] = a[s * seg_len:(s + 1) * seg_len]
        h_ref[s * seg_stride:s * seg_stride + seg_len, :] = b[s * seg_len:(s + 1) * seg_len]
    h = jnp.zeros((SUBLANES, LANES), F32)
    prod = jnp.ones((SUBLANES, LANES), F32)
    for t in range(seg_len):
        rows = pl.ds(t, SUBLANES, stride=seg_stride)
        at = a_ref[rows, :]
        h = at * h + h_ref[rows, :]
        prod = at * prod
        h_ref[rows, :] = h
        a_ref[rows, :] = prod
    gate = _gelu_tanh(gate_ref[...])
    carry = jnp.zeros((1, LANES), F32)
    for s in range(SUBLANES):
        seg = slice(s * seg_stride, s * seg_stride + seg_len)
        out = slice(s * seg_len, (s + 1) * seg_len)
        hs = h_ref[seg, :] + a_ref[seg, :] * carry
        o_ref[out, :] = (gate[out, :] * hs).astype(BF16)
        carry = h[s:s + 1, :] + prod[s:s + 1, :] * carry


def _lru(p, cw, cb, wa, ba, wx, bx, lam, *, batch, seq, col0):
    m = p.shape[0]
    width = cw.shape[1]
    ng = width // LANES
    blk = (seq, LANES)
    vec = pl.BlockSpec((1, LANES), lambda b, g: (0, g))
    mat = pl.BlockSpec((1, LANES, LANES), lambda b, g: (g, 0, 0))
    seg_stride = seq // SUBLANES + SUBLANES
    scan_buf = (SUBLANES * seg_stride, LANES)
    vmem = (4 * _nbytes(blk, F32) + 2 * _nbytes(blk, BF16) + 12 * _nbytes(blk, F32)
            + 2 * _nbytes(scan_buf, F32))
    return pl.pallas_call(
        _lru_kernel,
        grid=(batch, ng),
        in_specs=[
            pl.BlockSpec(blk, lambda b, g: (b, col0 + g)),
            pl.BlockSpec(blk, lambda b, g: (b, col0 + ng + g)),
            pl.BlockSpec((LRU_CONV_W, LANES), lambda b, g: (0, g)),
            vec, mat, vec, mat, vec, vec,
        ],
        out_specs=pl.BlockSpec(blk, lambda b, g: (b, g)),
        out_shape=jax.ShapeDtypeStruct((m, width), BF16),
        scratch_shapes=[pltpu.VMEM(scan_buf, F32), pltpu.VMEM(scan_buf, F32)],
        compiler_params=_params(("parallel", "parallel"), vmem),
        name="rg_lru",
    )(p, p, cw, cb, wa, ba, wx, bx, lam)


def _gate_merge_kernel(xn_ref, oa_ref, ob_ref, oc_ref, od_ref, wg_ref, bg_ref, wb_ref, y_ref):
    xn = xn_ref[...]
    y = None
    for k, o_ref in enumerate((oa_ref, ob_ref, oc_ref, od_ref)):
        gate = _sigmoid(_dot(xn, wg_ref[k]) + bg_ref[k])
        t = gate * _dot(o_ref[...], wb_ref[k])
        y = t if y is None else y + t
    y_ref[...] = y.astype(BF16)


def _gate_merge(xn, outs, wg, bg, wb, l, *, tm=1024, slabs=2):
    m, d = xn.shape
    bw = outs[0].shape[1]
    nb = len(outs)
    tn = d // slabs
    vmem = (_nbytes((nb, d, tn), BF16) + _nbytes((nb, bw, tn), BF16)
            + 2 * _nbytes((tm, d), BF16) + 2 * nb * _nbytes((tm, bw), BF16)
            + 2 * _nbytes((tm, tn), BF16) + 3 * _nbytes((tm, tn), F32))
    o_spec = pl.BlockSpec((tm, bw), lambda s, i: (i, 0))
    return pl.pallas_call(
        _gate_merge_kernel,
        grid=(slabs, m // tm),
        in_specs=[
            pl.BlockSpec((tm, d), lambda s, i: (i, 0)),
            o_spec, o_spec, o_spec, o_spec,
            _resident((None, nb, d, tn), lambda s, i: (l, 0, 0, s)),
            _resident((nb, 1, tn), lambda s, i: (0, 0, s)),
            _resident((None, nb, bw, tn), lambda s, i: (l, 0, 0, s)),
        ],
        out_specs=pl.BlockSpec((tm, tn), lambda s, i: (i, s)),
        out_shape=jax.ShapeDtypeStruct((m, d), BF16),
        compiler_params=_params(("arbitrary", "arbitrary"), vmem),
        name="gate_merge",
    )(xn, *outs, wg, bg, wb)


def _residual_proj_kernel(x_ref, a_ref, w_ref, g_ref, *out_refs, final):
    x_new = x_ref[...] + _dot(a_ref[...], w_ref[...])
    if final:
        out_refs[0][...] = _rmsnorm(x_new, g_ref[...])
    else:
        out_refs[0][...] = x_new
        out_refs[1][...] = _rmsnorm(x_new, g_ref[...]).astype(BF16)


def _residual_proj(x, a, w, l, g, *, final, tm, name):
    m, d = x.shape
    k = a.shape[1]
    tile = pl.BlockSpec((tm, d), lambda i: (i, 0))
    vmem = (_nbytes((k, d), BF16) + 2 * _nbytes((tm, k), BF16) + 5 * _nbytes((tm, d), F32)
            + 2 * _nbytes((tm, d), BF16))
    x_out = jax.ShapeDtypeStruct((m, d), F32)
    return pl.pallas_call(
        functools.partial(_residual_proj_kernel, final=final),
        grid=(m // tm,),
        in_specs=[
            tile,
            pl.BlockSpec((tm, k), lambda i: (i, 0)),
            _resident((None, k, d), lambda i: (l, 0, 0)),
            pl.BlockSpec((1, d), lambda i: (0, 0)),
        ],
        out_specs=tile if final else (tile, tile),
        out_shape=x_out if final else (x_out, jax.ShapeDtypeStruct((m, d), BF16)),
        compiler_params=_params(("parallel",), vmem),
        name=name,
    )(x, a, w, g)


def _ffn_up_kernel(xn_ref, w1_ref, w3_ref, h_ref, *, tc):
    xn = xn_ref[...]
    for c in range(0, h_ref.shape[1], tc):
        h1 = _dot(xn, w1_ref[:, c:c + tc])
        h3 = _dot(xn, w3_ref[:, c:c + tc])
        h_ref[:, c:c + tc] = ((h1 * _sigmoid(h1)) * h3).astype(BF16)


def _ffn_up(xn, w1, w3, l, *, tm=1024, slabs=2, tc=MXU_COLS):
    m, d = xn.shape
    f = w1.shape[2]
    tf = f // slabs
    assert tf % tc == 0
    vmem = (2 * _nbytes((d, tf), BF16) + 2 * _nbytes((tm, d), BF16)
            + 2 * _nbytes((tm, tf), BF16) + 6 * _nbytes((tm, tc), F32))
    w_spec = _resident((None, d, tf), lambda s, i: (l, 0, s))
    return pl.pallas_call(
        functools.partial(_ffn_up_kernel, tc=tc),
        grid=(slabs, m // tm),
        in_specs=[pl.BlockSpec((tm, d), lambda s, i: (i, 0)), w_spec, w_spec],
        out_specs=pl.BlockSpec((tm, tf), lambda s, i: (i, s)),
        out_shape=jax.ShapeDtypeStruct((m, f), BF16),
        compiler_params=_params(("arbitrary", "arbitrary"), vmem),
        name="ffn_up",
    )(xn, w1, w3)


def _rope_tables(seq, dh):
    half = dh // 2
    inv = ROPE_THETA ** (-jnp.arange(half, dtype=F32) / half)
    ang = jnp.arange(seq).astype(F32)[:, None] * inv[None, :]
    cos, sin = jnp.cos(ang), jnp.sin(ang)
    return jnp.concatenate([cos, cos], axis=-1), jnp.concatenate([-sin, sin], axis=-1)


def kernel(x, g_mix, w_in, w_sgu, b_sgu, g_sgu, w_sconv, w_lru_conv, b_lru_conv, w_lru_a,
           b_lru_a, w_lru_x, b_lru_x, lru_lambda, w_gate, b_gate, w_branch, w_out, g_ffn,
           w_ffn1, w_ffn3, w_ffn2, g_final):
    batch, seq, d = x.shape
    depth = w_in.shape[0]
    bw = w_branch.shape[2]
    heads = bw // LANES
    assert bw == N_GROUPS * LANES and seq % MOBA_BLOCK == 0 and w_sgu.shape[-1] == CHUNK
    col_b = 2 * bw // LANES
    col_c = 5 * bw // LANES
    col_d = 8 * bw // LANES

    cos, sin_signed = _rope_tables(seq, LANES)
    row = lambda v: v.reshape(1, -1)

    w_in, w_gate, w_branch, w_out, w_ffn1, w_ffn3, w_ffn2 = (
        w.astype(BF16) for w in (w_in, w_gate, w_branch, w_out, w_ffn1, w_ffn3, w_ffn2))

    h = x.reshape(batch * seq, d)
    hn = _norm(h, row(g_mix[0]))
    for l in range(depth):
        last = l == depth - 1
        qkv, o_a, o_b, o_d = _in_proj_mix(
            hn, w_in, l,
            (w_sgu[l], b_sgu[l].T[:, :, None], row(g_sgu[l])),
            w_sconv[l],
            (w_lru_conv[l], row(b_lru_conv[l]), w_lru_a[l].astype(BF16), row(b_lru_a[l]),
             w_lru_x[l].astype(BF16), row(b_lru_x[l]), row(lru_lambda[l])),
            seq=seq, bw=bw)
        o_c = _moba(qkv, cos, sin_signed, batch=batch, seq=seq, heads=heads, col0=0)
        y = _gate_merge(hn, (o_a, o_b, o_c, o_d), w_gate, b_gate[l][:, None, :], w_branch, l)
        h, hn = _residual_proj(h, y, w_out, l, row(g_ffn[l]), final=False,
                               tm=512, name="out_proj")
        hid = _ffn_up(hn, w_ffn1, w_ffn3, l)
        if last:
            h = _residual_proj(h, hid, w_ffn2, l, row(g_final), final=True,
                               tm=256, name="ffn_down_final")
        else:
            h, hn = _residual_proj(h, hid, w_ffn2, l, row(g_mix[l + 1]),
                                   final=False, tm=256, name="ffn_down")
    return h.reshape(batch, seq, d)
```

```python
import functools

import jax
import jax.numpy as jnp
import numpy as np
from jax import lax
from jax.experimental import pallas as pl
from jax.experimental.pallas import tpu as pltpu

N_GROUPS = 4
CHUNK = 128
SCONV_W = 3
MOBA_BLOCK = 256
MOBA_TOPK = 3
ROPE_THETA = 10000.0
NEG = -1e30
LRU_CONV_W = 4
LRU_C = 8.0
EPS = 1e-6

LANES = 128
SUBLANES = 8
MXU_COLS = 256
V7X_VMEM_BYTES = 64 * 1024 * 1024
VMEM_BUDGET_BYTES = V7X_VMEM_BYTES - 8 * 1024 * 1024

F32 = jnp.float32
BF16 = jnp.bfloat16

_SQRT_2_OVER_PI = float(np.sqrt(2.0 / np.pi).astype(np.float32))
_LOG2_E = 1.4426950408889634


def _params(semantics, vmem_bytes):
    assert vmem_bytes <= VMEM_BUDGET_BYTES, vmem_bytes
    return pltpu.CompilerParams(dimension_semantics=semantics,
                                vmem_limit_bytes=VMEM_BUDGET_BYTES)


def _nbytes(shape, dtype):
    return int(np.prod(shape)) * jnp.dtype(dtype).itemsize


def _resident(shape, index_map):
    return pl.BlockSpec(shape, index_map, pipeline_mode=pl.Buffered(1))


def _rmsnorm(x, g):
    var = jnp.mean(x * x, axis=-1, keepdims=True)
    return x * lax.rsqrt(var + EPS) * g


def _gelu_tanh(x):
    return x * (0.5 * (1.0 + jnp.tanh(_SQRT_2_OVER_PI * (x + 0.044715 * (x * x * x)))))


def _sigmoid(x):
    return 0.5 * (jnp.tanh(0.5 * x) + 1.0)


def _softplus(x):
    return jnp.maximum(x, 0.0) + jnp.log1p(jnp.exp(-jnp.abs(x)))


def _shift_rows(x, d, halo):
    row = lax.broadcasted_iota(jnp.int32, x.shape, 0)
    pad = jnp.zeros((x.shape[0] - SUBLANES, x.shape[1]), x.dtype)
    fill = jnp.concatenate([pltpu.roll(halo, d, axis=0), pad], axis=0)
    return jnp.where(row >= d, pltpu.roll(x, d, axis=0), fill)


def _causal_conv(x, w_ref, width, halo):
    acc = None
    for k in range(width):
        d = width - 1 - k
        term = (_shift_rows(x, d, halo) if d else x) * w_ref[k:k + 1, :]
        acc = term if acc is None else acc + term
    return acc


def _dot(a, b):
    return jnp.dot(a, b, preferred_element_type=F32)


def _dot_nt(a, b):
    return lax.dot_general(a, b, (((1,), (1,)), ((), ())), preferred_element_type=F32)


def _split_bf16(x):
    hi = x.astype(BF16)
    lo = (x - hi.astype(F32)).astype(BF16)
    return hi, lo


def _norm_kernel(x_ref, g_ref, xn_ref):
    xn_ref[...] = _rmsnorm(x_ref[...], g_ref[...]).astype(BF16)


def _norm(x, g, *, tm=512):
    m, d = x.shape
    vmem = 2 * _nbytes((tm, d), F32) + 2 * _nbytes((tm, d), BF16)
    return pl.pallas_call(
        _norm_kernel,
        grid=(m // tm,),
        in_specs=[pl.BlockSpec((tm, d), lambda i: (i, 0)), pl.BlockSpec((1, d), lambda i: (0, 0))],
        out_specs=pl.BlockSpec((tm, d), lambda i: (i, 0)),
        out_shape=jax.ShapeDtypeStruct((m, d), BF16),
        compiler_params=_params(("parallel",), vmem),
        name="norm",
    )(x, g)


def _in_proj_mix_kernel(xn_ref, w_ref, wsgu_ref, bsgu_ref, gsgu_ref, wsc_ref,
                        cw_ref, cb_ref, wa_ref, ba_ref, wx_ref, bx_ref, lam_ref,
                        qkv_ref, oa_ref, ob_ref, od_ref,
                        zprev_ref, xprev_ref, hcarry_ref, a_ref, h_ref, *, tiles_per_seq):
    tm, bw = oa_ref.shape

    @pl.when(pl.program_id(0) % tiles_per_seq == 0)
    def _():
        zprev_ref[...] = jnp.zeros(zprev_ref.shape, F32)
        xprev_ref[...] = jnp.zeros(xprev_ref.shape, F32)
        hcarry_ref[...] = jnp.zeros(hcarry_ref.shape, F32)

    xn = xn_ref[...]

    def proj(c):
        return _dot(xn, w_ref[:, c * bw:(c + 1) * bw])

    groups = [slice(g * LANES, (g + 1) * LANES) for g in range(N_GROUPS)]

    xr, gate_in = proj(8), proj(9)

    xhalo = xprev_ref[...]
    xprev_ref[...] = xr[tm - SUBLANES:, :]
    xc = _causal_conv(xr, cw_ref, LRU_CONV_W, xhalo) + cb_ref[...]
    xb = xc.astype(BF16)

    u, v = proj(0), proj(1)

    r_pre = jnp.concatenate([_dot(xb[:, cs], wa_ref[g]) for g, cs in enumerate(groups)], axis=1)
    i_pre = jnp.concatenate([_dot(xb[:, cs], wx_ref[g]) for g, cs in enumerate(groups)], axis=1)

    vn = _rmsnorm(_gelu_tanh(v), gsgu_ref[...]).astype(BF16)
    gu = _gelu_tanh(u)

    for c0 in range(0, bw, MXU_COLS):
        cs = slice(c0, c0 + MXU_COLS)
        c_gate, x_conv, b_gate = (_dot(xn, w_ref[:, c * bw + c0:c * bw + c0 + MXU_COLS])
                                  for c in (3, 4, 2))
        z = c_gate * x_conv
        zhalo = zprev_ref[:, cs]
        zprev_ref[:, cs] = z[tm - SUBLANES:, :]
        ob_ref[:, cs] = (b_gate * _causal_conv(z, wsc_ref.at[:, cs], SCONV_W, zhalo)).astype(BF16)

    row = lax.broadcasted_iota(jnp.int32, (CHUNK, CHUNK), 0)
    col = lax.broadcasted_iota(jnp.int32, (CHUNK, CHUNK), 1)
    for g, cs in enumerate(groups):
        w = jnp.where(col <= row, wsgu_ref[g], 0.0).astype(BF16)
        bias = bsgu_ref[g]
        for n in range(tm // CHUNK):
            rs = slice(n * CHUNK, (n + 1) * CHUNK)
            oa_ref[rs, cs] = (gu[rs, cs] * (_dot(w, vn[rs, cs]) + bias)).astype(BF16)

    r = _sigmoid(r_pre + ba_ref[...])
    i = _sigmoid(i_pre + bx_ref[...])
    log_a = (-LRU_C * r) * _softplus(-lam_ref[...])
    a = jnp.exp(log_a)
    b = jnp.sqrt(jnp.tanh(-log_a) * (1.0 + a * a)) * (i * xc)

    qkv_ref[:, 0:bw] = proj(5)

    seg_len = tm // SUBLANES
    seg_stride = a_ref.shape[1] // SUBLANES
    for g, cs in enumerate(groups):
        for s in range(SUBLANES):
            dst = slice(s * seg_stride, s * seg_stride + seg_len)
            a_ref[g, dst, :] = a[s * seg_len:(s + 1) * seg_len, cs]
            h_ref[g, dst, :] = b[s * seg_len:(s + 1) * seg_len, cs]
    h = [jnp.zeros((SUBLANES, LANES), F32)] * N_GROUPS
    prod = [jnp.ones((SUBLANES, LANES), F32)] * N_GROUPS
    for t in range(seg_len):
        rows = pl.ds(t, SUBLANES, stride=seg_stride)
        for g in range(N_GROUPS):
            at = a_ref[g, rows, :]
            h[g] = at * h[g] + h_ref[g, rows, :]
            prod[g] = at * prod[g]
            h_ref[g, rows, :] = h[g]
            a_ref[g, rows, :] = prod[g]

    qkv_ref[:, bw:2 * bw] = proj(6)

    gate = _gelu_tanh(gate_in)
    for g, cs in enumerate(groups):
        carry = hcarry_ref[0:1, cs]
        for s in range(SUBLANES):
            seg = slice(s * seg_stride, s * seg_stride + seg_len)
            out = slice(s * seg_len, (s + 1) * seg_len)
            hs = h_ref[g, seg, :] + a_ref[g, seg, :] * carry
            od_ref[out, cs] = (gate[out, cs] * hs).astype(BF16)
            carry = h[g][s:s + 1, :] + prod[g][s:s + 1, :] * carry
        hcarry_ref[0:1, cs] = carry

    qkv_ref[:, 2 * bw:3 * bw] = proj(7)


def _in_proj_mix(xn, w, l, sgu, sconv_w, lru, *, seq, bw, tm=512):
    m, d = xn.shape
    n = w.shape[2]
    assert seq % tm == 0 and n == 10 * bw
    seg_stride = tm // SUBLANES + SUBLANES
    scan_buf = (bw // LANES, SUBLANES * seg_stride, LANES)
    halo = (SUBLANES, bw)
    vmem = (_nbytes((d, n), BF16) + 2 * _nbytes((tm, d), BF16) + 2 * _nbytes((tm, 3 * bw), F32)
            + 6 * _nbytes((tm, bw), BF16) + 2 * _nbytes(scan_buf, F32) + 16 * _nbytes((tm, bw), F32))
    full = lambda a: pl.BlockSpec(a.shape, lambda i, nd=a.ndim: (0,) * nd)
    small = (*sgu, sconv_w, *lru)
    o_spec = pl.BlockSpec((tm, bw), lambda i: (i, 0))
    o_shape = jax.ShapeDtypeStruct((m, bw), BF16)
    return pl.pallas_call(
        functools.partial(_in_proj_mix_kernel, tiles_per_seq=seq // tm),
        grid=(m // tm,),
        in_specs=[
            pl.BlockSpec((tm, d), lambda i: (i, 0)),
            _resident((None, d, n), lambda i: (l, 0, 0)),
            *[full(a) for a in small],
        ],
        out_specs=(pl.BlockSpec((tm, 3 * bw), lambda i: (i, 0)), o_spec, o_spec, o_spec),
        out_shape=(jax.ShapeDtypeStruct((m, 3 * bw), F32), o_shape, o_shape, o_shape),
        scratch_shapes=[pltpu.VMEM(halo, F32), pltpu.VMEM(halo, F32), pltpu.VMEM(halo, F32),
                        pltpu.VMEM(scan_buf, F32), pltpu.VMEM(scan_buf, F32)],
        compiler_params=_params(("arbitrary",), vmem),
        name="in_proj_mix",
    )(xn, w, *small)


def _rope(x, cos, sin_signed):
    return x * cos + pltpu.roll(x, x.shape[1] // 2, axis=1) * sin_signed


def _moba_kernel(q_ref, k_ref, v_ref, cos_ref, sin_ref, o_ref,
                 kr_ref, va_ref, km_ref, s_ref):
    seq = q_ref.shape[0]
    dh = LANES
    heads = range(q_ref.shape[1] // dh)
    lanes = [slice(h * dh, (h + 1) * dh) for h in heads]
    blk = MOBA_BLOCK
    nblk = seq // blk
    sub = SUBLANES
    assert nblk <= sub and blk == 2 * dh
    q_scale = dh ** -0.5 * _LOG2_E

    km_ref[...] = jnp.zeros(km_ref.shape, F32)
    for n in range(nblk):
        rs = slice(n * blk, (n + 1) * blk)
        for h in heads:
            kr = _rope(k_ref[rs, lanes[h]], cos_ref[rs, :], sin_ref[rs, :])
            kr_ref[h, rs, :] = kr.astype(BF16)
            km_ref[h, n:n + 1, :] = jnp.sum(kr, axis=0, keepdims=True) * (1.0 / blk)
            va_ref[h, rs, :dh] = v_ref[rs, lanes[h]].astype(BF16)
            va_ref[h, rs, dh:] = jnp.ones((blk, dh), BF16)
    km_parts = []
    for h in heads:
        km_hi, km_lo = _split_bf16(km_ref[h])
        km_parts.append((jnp.concatenate([km_hi, km_hi], axis=1), km_lo))

    row = lax.broadcasted_iota(jnp.int32, (blk, blk), 0)
    col = lax.broadcasted_iota(jnp.int32, (blk, blk), 1)
    causal = col <= row
    blk_id = lax.broadcasted_iota(jnp.int32, (sub, blk), 0)
    er = lax.broadcasted_iota(jnp.int32, (LANES, nblk * dh), 0)
    ec = lax.broadcasted_iota(jnp.int32, (LANES, nblk * dh), 1)
    expand = jnp.where(er == ec // dh, 1.0, 0.0).astype(BF16)

    def rotary_queries(h, i):
        if i >= nblk:
            return None, None
        rs = slice(i * blk, (i + 1) * blk)
        qr = _rope(q_ref[rs, lanes[h]], cos_ref[rs, :], sin_ref[rs, :])
        qs = (qr * q_scale).astype(BF16)
        if i <= MOBA_TOPK:
            return qs, None
        q_hi, q_lo = _split_bf16(qr)
        q_hl = jnp.concatenate([q_hi, q_lo], axis=1)
        km_hh, km_lo = km_parts[h]
        return qs, (_dot_nt(km_hh, q_hl) + _dot_nt(km_lo, q_hi))[:sub, :]

    def row_masks(i, gate):
        if gate is None:
            return None
        cnt = jnp.zeros(gate.shape, F32)
        for m in range(i):
            gm = gate[m:m + 1, :]
            beats = (gm > gate) | ((gm == gate) & (blk_id > m))
            cnt = cnt + jnp.where(beats, 1.0, 0.0)
        sel_t = jnp.where(cnt < float(MOBA_TOPK), 1.0, 0.0)
        sel_t = jnp.concatenate([sel_t, jnp.zeros((LANES - sub, blk), F32)], axis=0)
        sel = sel_t.T.astype(BF16)
        keep = [None] * i
        for n in range(0, i, 2):
            u = _dot(sel, expand[:, n * dh:(n + 2) * dh])
            keep[n] = jnp.concatenate([u[:, :dh], u[:, :dh]], axis=1) > 0.5
            if n + 1 < i:
                keep[n + 1] = jnp.concatenate([u[:, dh:], u[:, dh:]], axis=1) > 0.5
        return keep

    def all_scores(h, i, qs, keep):
        if i >= nblk:
            return None
        slot = 2 * h + i % 2
        m_tile = None
        for j in range(i + 1):
            ks = slice(j * blk, (j + 1) * blk)
            s = _dot_nt(qs, kr_ref[h, ks, :])
            if j == i:
                s = jnp.where(causal, s, NEG)
            elif keep is not None:
                s = jnp.where(keep[j], s, NEG)
            s_ref[slot, :, ks] = s
            mj = jnp.maximum(s[:, :dh], s[:, dh:])
            m_tile = mj if m_tile is None else jnp.maximum(m_tile, mj)
        return jnp.max(m_tile, axis=-1, keepdims=True)

    def attend(h, i, m_row):
        slot = 2 * h + i % 2
        acc = jnp.zeros((blk, 2 * dh), F32)
        for j in range(i + 1):
            ks = slice(j * blk, (j + 1) * blk)
            p = jnp.exp2(s_ref[slot, :, ks] - m_row)
            acc = acc + _dot(p.astype(BF16), va_ref[h, ks, :])
        o_ref[i * blk:(i + 1) * blk, lanes[h]] = (acc[:, :dh] / acc[:, dh:]).astype(BF16)

    q0 = [rotary_queries(h, 0) for h in heads]
    m_row = [all_scores(h, 0, q0[h][0], row_masks(0, q0[h][1])) for h in heads]
    q1 = [rotary_queries(h, 1) for h in heads]
    keep1 = [row_masks(1, q1[h][1]) for h in heads]
    for i in range(nblk):
        q2 = [rotary_queries(h, i + 2) for h in heads]
        m_next = [all_scores(h, i + 1, q1[h][0], keep1[h]) for h in heads]
        keep2 = [row_masks(i + 2, q2[h][1]) for h in heads]
        for h in heads:
            attend(h, i, m_row[h])
        m_row, q1, keep1 = m_next, q2, keep2


def _moba(qkv, cos, sin_signed, *, batch, seq, heads, group=2):
    m = qkv.shape[0]
    dh = LANES
    assert heads % group == 0
    ng = heads // group
    blk = (seq, group * dh)
    tab = (seq, dh)
    vmem = (6 * _nbytes(blk, F32) + 4 * _nbytes(tab, F32) + 2 * _nbytes(blk, BF16)
            + 3 * _nbytes(blk, BF16) + group * _nbytes((LANES, dh), F32)
            + 2 * group * _nbytes((MOBA_BLOCK, seq), F32))
    return pl.pallas_call(
        _moba_kernel,
        grid=(batch, ng),
        in_specs=[
            pl.BlockSpec(blk, lambda b, g: (b, g)),
            pl.BlockSpec(blk, lambda b, g: (b, ng + g)),
            pl.BlockSpec(blk, lambda b, g: (b, 2 * ng + g)),
            pl.BlockSpec(tab, lambda b, g: (0, 0)),
            pl.BlockSpec(tab, lambda b, g: (0, 0)),
        ],
        out_specs=pl.BlockSpec(blk, lambda b, g: (b, g)),
        out_shape=jax.ShapeDtypeStruct((m, heads * dh), BF16),
        scratch_shapes=[
            pltpu.VMEM((group, seq, dh), BF16),
            pltpu.VMEM((group, seq, 2 * dh), BF16),
            pltpu.VMEM((group, LANES, dh), F32),
            pltpu.VMEM((2 * group, MOBA_BLOCK, seq), F32),
        ],
        compiler_params=_params(("parallel", "parallel"), vmem),
        name="moba",
    )(qkv, qkv, qkv, cos, sin_signed)


def _gate_merge_kernel(xn_ref, oa_ref, ob_ref, oc_ref, od_ref, wg_ref, bg_ref, wb_ref, y_ref, *, tc):
    xn = xn_ref[...]
    outs = [o_ref[...] for o_ref in (oa_ref, ob_ref, oc_ref, od_ref)]
    for c in range(0, y_ref.shape[1], tc):
        cs = slice(c, c + tc)
        y = None
        for k, o in enumerate(outs):
            gate = _sigmoid(_dot(xn, wg_ref[k, :, cs]) + bg_ref[k, :, cs])
            t = gate * _dot(o, wb_ref[k, :, cs])
            y = t if y is None else y + t
        y_ref[:, cs] = y.astype(BF16)


def _gate_merge(xn, outs, wg, bg, wb, l, *, tm=1024, slabs=2, tc=MXU_COLS):
    m, d = xn.shape
    bw = outs[0].shape[1]
    nb = len(outs)
    tn = d // slabs
    vmem = (_nbytes((nb, d, tn), BF16) + _nbytes((nb, bw, tn), BF16)
            + 2 * _nbytes((tm, d), BF16) + 2 * nb * _nbytes((tm, bw), BF16)
            + 2 * _nbytes((tm, tn), BF16) + 6 * _nbytes((tm, tc), F32))
    o_spec = pl.BlockSpec((tm, bw), lambda s, i: (i, 0))
    return pl.pallas_call(
        functools.partial(_gate_merge_kernel, tc=tc),
        grid=(slabs, m // tm),
        in_specs=[
            pl.BlockSpec((tm, d), lambda s, i: (i, 0)),
            o_spec, o_spec, o_spec, o_spec,
            _resident((None, nb, d, tn), lambda s, i: (l, 0, 0, s)),
            _resident((nb, 1, tn), lambda s, i: (0, 0, s)),
            _resident((None, nb, bw, tn), lambda s, i: (l, 0, 0, s)),
        ],
        out_specs=pl.BlockSpec((tm, tn), lambda s, i: (i, s)),
        out_shape=jax.ShapeDtypeStruct((m, d), BF16),
        compiler_params=_params(("arbitrary", "arbitrary"), vmem),
        name="gate_merge",
    )(xn, *outs, wg, bg, wb)


def _residual_proj_kernel(x_ref, a_ref, w_ref, g_ref, *out_refs, final):
    x_new = x_ref[...] + _dot(a_ref[...], w_ref[...])
    if final:
        out_refs[0][...] = _rmsnorm(x_new, g_ref[...])
    else:
        out_refs[0][...] = x_new
        out_refs[1][...] = _rmsnorm(x_new, g_ref[...]).astype(BF16)


def _residual_proj(x, a, w, l, g, *, final, tm, name):
    m, d = x.shape
    k = a.shape[1]
    tile = pl.BlockSpec((tm, d), lambda i: (i, 0))
    vmem = (_nbytes((k, d), BF16) + 2 * _nbytes((tm, k), BF16) + 5 * _nbytes((tm, d), F32)
            + 2 * _nbytes((tm, d), BF16))
    x_out = jax.ShapeDtypeStruct((m, d), F32)
    return pl.pallas_call(
        functools.partial(_residual_proj_kernel, final=final),
        grid=(m // tm,),
        in_specs=[
            tile,
            pl.BlockSpec((tm, k), lambda i: (i, 0)),
            _resident((None, k, d), lambda i: (l, 0, 0)),
            pl.BlockSpec((1, d), lambda i: (0, 0)),
        ],
        out_specs=tile if final else (tile, tile),
        out_shape=x_out if final else (x_out, jax.ShapeDtypeStruct((m, d), BF16)),
        compiler_params=_params(("parallel",), vmem),
        name=name,
    )(x, a, w, g)


def _ffn_up_kernel(xn_ref, w1_ref, w3_ref, h_ref, *, tc):
    xn = xn_ref[...]
    for c in range(0, h_ref.shape[1], tc):
        h1 = _dot(xn, w1_ref[:, c:c + tc])
        h3 = _dot(xn, w3_ref[:, c:c + tc])
        h_ref[:, c:c + tc] = ((h1 * _sigmoid(h1)) * h3).astype(BF16)


def _ffn_up(xn, w1, w3, l, *, tm=1024, slabs=2, tc=MXU_COLS):
    m, d = xn.shape
    f = w1.shape[2]
    tf = f // slabs
    assert tf % tc == 0
    vmem = (2 * _nbytes((d, tf), BF16) + 2 * _nbytes((tm, d), BF16)
            + 2 * _nbytes((tm, tf), BF16) + 6 * _nbytes((tm, tc), F32))
    w_spec = _resident((None, d, tf), lambda s, i: (l, 0, s))
    return pl.pallas_call(
        functools.partial(_ffn_up_kernel, tc=tc),
        grid=(slabs, m // tm),
        in_specs=[pl.BlockSpec((tm, d), lambda s, i: (i, 0)), w_spec, w_spec],
        out_specs=pl.BlockSpec((tm, tf), lambda s, i: (i, s)),
        out_shape=jax.ShapeDtypeStruct((m, f), BF16),
        compiler_params=_params(("arbitrary", "arbitrary"), vmem),
        name="ffn_up",
    )(xn, w1, w3)


def _rope_tables(seq, dh):
    half = dh // 2
    inv = ROPE_THETA ** (-jnp.arange(half, dtype=F32) / half)
    ang = jnp.arange(seq).astype(F32)[:, None] * inv[None, :]
    cos, sin = jnp.cos(ang), jnp.sin(ang)
    return jnp.concatenate([cos, cos], axis=-1), jnp.concatenate([-sin, sin], axis=-1)


def kernel(x, g_mix, w_in, w_sgu, b_sgu, g_sgu, w_sconv, w_lru_conv, b_lru_conv, w_lru_a,
           b_lru_a, w_lru_x, b_lru_x, lru_lambda, w_gate, b_gate, w_branch, w_out, g_ffn,
           w_ffn1, w_ffn3, w_ffn2, g_final):
    batch, seq, d = x.shape
    depth = w_in.shape[0]
    bw = w_branch.shape[2]
    heads = bw // LANES
    assert bw == N_GROUPS * LANES and seq % MOBA_BLOCK == 0 and w_sgu.shape[-1] == CHUNK

    cos, sin_signed = _rope_tables(seq, LANES)
    row = lambda v: v.reshape(1, -1)

    w_in, w_gate, w_branch, w_out, w_ffn1, w_ffn3, w_ffn2 = (
        w.astype(BF16) for w in (w_in, w_gate, w_branch, w_out, w_ffn1, w_ffn3, w_ffn2))

    h = x.reshape(batch * seq, d)
    hn = _norm(h, row(g_mix[0]))
    for l in range(depth):
        last = l == depth - 1
        qkv, o_a, o_b, o_d = _in_proj_mix(
            hn, w_in, l,
            (w_sgu[l], b_sgu[l].T[:, :, None], row(g_sgu[l])),
            w_sconv[l],
            (w_lru_conv[l], row(b_lru_conv[l]), w_lru_a[l].astype(BF16), row(b_lru_a[l]),
             w_lru_x[l].astype(BF16), row(b_lru_x[l]), row(lru_lambda[l])),
            seq=seq, bw=bw)
        o_c = _moba(qkv, cos, sin_signed, batch=batch, seq=seq, heads=heads)
        y = _gate_merge(hn, (o_a, o_b, o_c, o_d), w_gate, b_gate[l][:, None, :], w_branch, l)
        h, hn = _residual_proj(h, y, w_out, l, row(g_ffn[l]), final=False,
                               tm=512, name="out_proj")
        hid = _ffn_up(hn, w_ffn1, w_ffn3, l)
        if last:
            h = _residual_proj(h, hid, w_ffn2, l, row(g_final), final=True,
                               tm=256, name="ffn_down_final")
        else:
            h, hn = _residual_proj(h, hid, w_ffn2, l, row(g_mix[l + 1]),
                                   final=False, tm=256, name="ffn_down")
    return h.reshape(batch, seq, d)
```

```python
import functools

import jax
import jax.numpy as jnp
import numpy as np
from jax import lax
from jax.experimental import pallas as pl
from jax.experimental.pallas import tpu as pltpu

N_GROUPS = 4
CHUNK = 128
SCONV_W = 3
MOBA_BLOCK = 256
MOBA_TOPK = 3
ROPE_THETA = 10000.0
NEG = -1e30
LRU_CONV_W = 4
LRU_C = 8.0
EPS = 1e-6

LANES = 128
SUBLANES = 8
MXU_COLS = 256
V7X_VMEM_BYTES = 64 * 1024 * 1024
VMEM_BUDGET_BYTES = V7X_VMEM_BYTES - 8 * 1024 * 1024

F32 = jnp.float32
BF16 = jnp.bfloat16

_SQRT_2_OVER_PI = float(np.sqrt(2.0 / np.pi).astype(np.float32))
_LOG2_E = 1.4426950408889634


def _params(semantics, vmem_bytes):
    assert vmem_bytes <= VMEM_BUDGET_BYTES, vmem_bytes
    return pltpu.CompilerParams(dimension_semantics=semantics,
                                vmem_limit_bytes=VMEM_BUDGET_BYTES)


def _nbytes(shape, dtype):
    return int(np.prod(shape)) * jnp.dtype(dtype).itemsize


def _resident(shape, index_map):
    return pl.BlockSpec(shape, index_map, pipeline_mode=pl.Buffered(1))


def _rmsnorm(x, g):
    var = jnp.mean(x * x, axis=-1, keepdims=True)
    return x * lax.rsqrt(var + EPS) * g


def _gelu_tanh(x):
    return x * (0.5 * (1.0 + jnp.tanh(_SQRT_2_OVER_PI * (x + 0.044715 * (x * x * x)))))


def _sigmoid(x):
    return 0.5 * (jnp.tanh(0.5 * x) + 1.0)


def _softplus(x):
    return jnp.maximum(x, 0.0) + jnp.log1p(jnp.exp(-jnp.abs(x)))


def _shift_rows(x, d, halo):
    row = lax.broadcasted_iota(jnp.int32, x.shape, 0)
    pad = jnp.zeros((x.shape[0] - SUBLANES, x.shape[1]), x.dtype)
    fill = jnp.concatenate([pltpu.roll(halo, d, axis=0), pad], axis=0)
    return jnp.where(row >= d, pltpu.roll(x, d, axis=0), fill)


def _causal_conv(x, w_ref, width, halo):
    acc = None
    for k in range(width):
        d = width - 1 - k
        term = (_shift_rows(x, d, halo) if d else x) * w_ref[k:k + 1, :]
        acc = term if acc is None else acc + term
    return acc


def _dot(a, b):
    return jnp.dot(a, b, preferred_element_type=F32)


def _dot_nt(a, b):
    return lax.dot_general(a, b, (((1,), (1,)), ((), ())), preferred_element_type=F32)


def _split_bf16(x):
    hi = x.astype(BF16)
    lo = (x - hi.astype(F32)).astype(BF16)
    return hi, lo


def _norm_kernel(x_ref, g_ref, xn_ref):
    xn_ref[...] = _rmsnorm(x_ref[...], g_ref[...]).astype(BF16)


def _norm(x, g, *, tm=512):
    m, d = x.shape
    vmem = 2 * _nbytes((tm, d), F32) + 2 * _nbytes((tm, d), BF16)
    return pl.pallas_call(
        _norm_kernel,
        grid=(m // tm,),
        in_specs=[pl.BlockSpec((tm, d), lambda i: (i, 0)), pl.BlockSpec((1, d), lambda i: (0, 0))],
        out_specs=pl.BlockSpec((tm, d), lambda i: (i, 0)),
        out_shape=jax.ShapeDtypeStruct((m, d), BF16),
        compiler_params=_params(("parallel",), vmem),
        name="norm",
    )(x, g)


def _in_proj_mix_kernel(xn_ref, w_ref, wsgu_ref, bsgu_ref, gsgu_ref, wsc_ref,
                        cw_ref, cb_ref, wa_ref, ba_ref, wx_ref, bx_ref, lam_ref,
                        qkv_ref, oa_ref, ob_ref, od_ref,
                        zprev_ref, xprev_ref, hcarry_ref, a_ref, h_ref, *, tiles_per_seq):
    tm, bw = oa_ref.shape

    @pl.when(pl.program_id(0) % tiles_per_seq == 0)
    def _():
        zprev_ref[...] = jnp.zeros(zprev_ref.shape, F32)
        xprev_ref[...] = jnp.zeros(xprev_ref.shape, F32)
        hcarry_ref[...] = jnp.zeros(hcarry_ref.shape, F32)

    xn = xn_ref[...]

    def proj(c):
        return _dot(xn, w_ref[:, c * bw:(c + 1) * bw])

    groups = [slice(g * LANES, (g + 1) * LANES) for g in range(N_GROUPS)]

    xr, gate_in = proj(8), proj(9)

    xhalo = xprev_ref[...]
    xprev_ref[...] = xr[tm - SUBLANES:, :]
    xc = _causal_conv(xr, cw_ref, LRU_CONV_W, xhalo) + cb_ref[...]
    xb = xc.astype(BF16)

    u, v = proj(0), proj(1)

    r_pre = jnp.concatenate([_dot(xb[:, cs], wa_ref[g]) for g, cs in enumerate(groups)], axis=1)
    i_pre = jnp.concatenate([_dot(xb[:, cs], wx_ref[g]) for g, cs in enumerate(groups)], axis=1)

    vn = _rmsnorm(_gelu_tanh(v), gsgu_ref[...]).astype(BF16)
    gu = _gelu_tanh(u)

    for c0 in range(0, bw, MXU_COLS):
        cs = slice(c0, c0 + MXU_COLS)
        c_gate, x_conv, b_gate = (_dot(xn, w_ref[:, c * bw + c0:c * bw + c0 + MXU_COLS])
                                  for c in (3, 4, 2))
        z = c_gate * x_conv
        zhalo = zprev_ref[:, cs]
        zprev_ref[:, cs] = z[tm - SUBLANES:, :]
        ob_ref[:, cs] = (b_gate * _causal_conv(z, wsc_ref.at[:, cs], SCONV_W, zhalo)).astype(BF16)

    row = lax.broadcasted_iota(jnp.int32, (CHUNK, CHUNK), 0)
    col = lax.broadcasted_iota(jnp.int32, (CHUNK, CHUNK), 1)
    for g, cs in enumerate(groups):
        w = jnp.where(col <= row, wsgu_ref[g], 0.0).astype(BF16)
        bias = bsgu_ref[g]
        for n in range(tm // CHUNK):
            rs = slice(n * CHUNK, (n + 1) * CHUNK)
            oa_ref[rs, cs] = (gu[rs, cs] * (_dot(w, vn[rs, cs]) + bias)).astype(BF16)

    r = _sigmoid(r_pre + ba_ref[...])
    i = _sigmoid(i_pre + bx_ref[...])
    log_a = (-LRU_C * r) * _softplus(-lam_ref[...])
    a = jnp.exp(log_a)
    b = jnp.sqrt(jnp.tanh(-log_a) * (1.0 + a * a)) * (i * xc)

    qkv_ref[:, 0:bw] = proj(5)

    seg_len = tm // SUBLANES
    seg_stride = a_ref.shape[1] // SUBLANES
    for g, cs in enumerate(groups):
        for s in range(SUBLANES):
            dst = slice(s * seg_stride, s * seg_stride + seg_len)
            a_ref[g, dst, :] = a[s * seg_len:(s + 1) * seg_len, cs]
            h_ref[g, dst, :] = b[s * seg_len:(s + 1) * seg_len, cs]
    h = [jnp.zeros((SUBLANES, LANES), F32)] * N_GROUPS
    prod = [jnp.ones((SUBLANES, LANES), F32)] * N_GROUPS
    for t in range(seg_len):
        rows = pl.ds(t, SUBLANES, stride=seg_stride)
        for g in range(N_GROUPS):
            at = a_ref[g, rows, :]
            h[g] = at * h[g] + h_ref[g, rows, :]
            prod[g] = at * prod[g]
            h_ref[g, rows, :] = h[g]
            a_ref[g, rows, :] = prod[g]

    qkv_ref[:, bw:2 * bw] = proj(6)

    gate = _gelu_tanh(gate_in)
    for g, cs in enumerate(groups):
        carry = hcarry_ref[0:1, cs]
        for s in range(SUBLANES):
            seg = slice(s * seg_stride, s * seg_stride + seg_len)
            out = slice(s * seg_len, (s + 1) * seg_len)
            hs = h_ref[g, seg, :] + a_ref[g, seg, :] * carry
            od_ref[out, cs] = (gate[out, cs] * hs).astype(BF16)
            carry = h[g][s:s + 1, :] + prod[g][s:s + 1, :] * carry
        hcarry_ref[0:1, cs] = carry

    qkv_ref[:, 2 * bw:3 * bw] = proj(7)


def _in_proj_mix(xn, w, l, sgu, sconv_w, lru, *, seq, bw, tm=512):
    m, d = xn.shape
    n = w.shape[2]
    assert seq % tm == 0 and n == 10 * bw
    seg_stride = tm // SUBLANES + SUBLANES
    scan_buf = (bw // LANES, SUBLANES * seg_stride, LANES)
    halo = (SUBLANES, bw)
    vmem = (_nbytes((d, n), BF16) + 2 * _nbytes((tm, d), BF16) + 2 * _nbytes((tm, 3 * bw), F32)
            + 6 * _nbytes((tm, bw), BF16) + 2 * _nbytes(scan_buf, F32) + 16 * _nbytes((tm, bw), F32))
    full = lambda a: pl.BlockSpec(a.shape, lambda i, nd=a.ndim: (0,) * nd)
    small = (*sgu, sconv_w, *lru)
    o_spec = pl.BlockSpec((tm, bw), lambda i: (i, 0))
    o_shape = jax.ShapeDtypeStruct((m, bw), BF16)
    return pl.pallas_call(
        functools.partial(_in_proj_mix_kernel, tiles_per_seq=seq // tm),
        grid=(m // tm,),
        in_specs=[
            pl.BlockSpec((tm, d), lambda i: (i, 0)),
            _resident((None, d, n), lambda i: (l, 0, 0)),
            *[full(a) for a in small],
        ],
        out_specs=(pl.BlockSpec((tm, 3 * bw), lambda i: (i, 0)), o_spec, o_spec, o_spec),
        out_shape=(jax.ShapeDtypeStruct((m, 3 * bw), F32), o_shape, o_shape, o_shape),
        scratch_shapes=[pltpu.VMEM(halo, F32), pltpu.VMEM(halo, F32), pltpu.VMEM(halo, F32),
                        pltpu.VMEM(scan_buf, F32), pltpu.VMEM(scan_buf, F32)],
        compiler_params=_params(("arbitrary",), vmem),
        name="in_proj_mix",
    )(xn, w, *small)


def _rope(x, cos, sin_signed):
    return x * cos + pltpu.roll(x, x.shape[1] // 2, axis=1) * sin_signed


def _moba_kernel(q_ref, k_ref, v_ref, cos_ref, sin_ref, o_ref,
                 kr_ref, va_ref, km_ref, s_ref):
    seq = q_ref.shape[0]
    dh = LANES
    heads = range(q_ref.shape[1] // dh)
    lanes = [slice(h * dh, (h + 1) * dh) for h in heads]
    blk = MOBA_BLOCK
    nblk = seq // blk
    sub = SUBLANES
    assert nblk <= sub and blk == 2 * dh
    q_scale = dh ** -0.5 * _LOG2_E

    km_ref[...] = jnp.zeros(km_ref.shape, F32)
    for n in range(nblk):
        rs = slice(n * blk, (n + 1) * blk)
        for h in heads:
            kr = _rope(k_ref[rs, lanes[h]], cos_ref[rs, :], sin_ref[rs, :])
            kr_ref[h, rs, :] = kr.astype(BF16)
            km_ref[h, n:n + 1, :] = jnp.sum(kr, axis=0, keepdims=True) * (1.0 / blk)
            va_ref[h, rs, :dh] = v_ref[rs, lanes[h]].astype(BF16)
            va_ref[h, rs, dh:] = jnp.ones((blk, dh), BF16)
    km_parts = []
    for h in heads:
        km_hi, km_lo = _split_bf16(km_ref[h])
        km_parts.append((jnp.concatenate([km_hi, km_hi], axis=1), km_lo))

    row = lax.broadcasted_iota(jnp.int32, (blk, blk), 0)
    col = lax.broadcasted_iota(jnp.int32, (blk, blk), 1)
    causal = col <= row
    blk_id = lax.broadcasted_iota(jnp.int32, (sub, blk), 0)
    er = lax.broadcasted_iota(jnp.int32, (LANES, nblk * dh), 0)
    ec = lax.broadcasted_iota(jnp.int32, (LANES, nblk * dh), 1)
    expand = jnp.where(er == ec // dh, 1.0, 0.0).astype(BF16)

    def rotary_queries(h, i):
        if i >= nblk:
            return None, None
        rs = slice(i * blk, (i + 1) * blk)
        qr = _rope(q_ref[rs, lanes[h]], cos_ref[rs, :], sin_ref[rs, :])
        qs = (qr * q_scale).astype(BF16)
        if i <= MOBA_TOPK:
            return qs, None
        q_hi, q_lo = _split_bf16(qr)
        q_hl = jnp.concatenate([q_hi, q_lo], axis=1)
        km_hh, km_lo = km_parts[h]
        return qs, (_dot_nt(km_hh, q_hl) + _dot_nt(km_lo, q_hi))[:sub, :]

    def row_masks(i, gate):
        if gate is None:
            return None
        cnt = jnp.zeros(gate.shape, F32)
        for m in range(i):
            gm = gate[m:m + 1, :]
            beats = (gm > gate) | ((gm == gate) & (blk_id > m))
            cnt = cnt + jnp.where(beats, 1.0, 0.0)
        sel_t = jnp.where(cnt < float(MOBA_TOPK), 1.0, 0.0)
        sel_t = jnp.concatenate([sel_t, jnp.zeros((LANES - sub, blk), F32)], axis=0)
        sel = sel_t.T.astype(BF16)
        keep = [None] * i
        for n in range(0, i, 2):
            u = _dot(sel, expand[:, n * dh:(n + 2) * dh])
            keep[n] = jnp.concatenate([u[:, :dh], u[:, :dh]], axis=1) > 0.5
            if n + 1 < i:
                keep[n + 1] = jnp.concatenate([u[:, dh:], u[:, dh:]], axis=1) > 0.5
        return keep

    def all_scores(h, i, qs, keep):
        if i >= nblk:
            return None
        slot = 2 * h + i % 2
        m_tile = None
        for j in range(i + 1):
            ks = slice(j * blk, (j + 1) * blk)
            s = _dot_nt(qs, kr_ref[h, ks, :])
            if j == i:
                s = jnp.where(causal, s, NEG)
            elif keep is not None:
                s = jnp.where(keep[j], s, NEG)
            s_ref[slot, :, ks] = s
            mj = jnp.maximum(s[:, :dh], s[:, dh:])
            m_tile = mj if m_tile is None else jnp.maximum(m_tile, mj)
        return jnp.max(m_tile, axis=-1, keepdims=True)

    def attend(h, i, m_row):
        slot = 2 * h + i % 2
        acc = jnp.zeros((blk, 2 * dh), F32)
        for j in range(i + 1):
            ks = slice(j * blk, (j + 1) * blk)
            p = jnp.exp2(s_ref[slot, :, ks] - m_row)
            acc = acc + _dot(p.astype(BF16), va_ref[h, ks, :])
        o_ref[i * blk:(i + 1) * blk, lanes[h]] = (acc[:, :dh] / acc[:, dh:]).astype(BF16)

    q0 = [rotary_queries(h, 0) for h in heads]
    m_row = [all_scores(h, 0, q0[h][0], row_masks(0, q0[h][1])) for h in heads]
    q1 = [rotary_queries(h, 1) for h in heads]
    keep1 = [row_masks(1, q1[h][1]) for h in heads]
    for i in range(nblk):
        q2 = [rotary_queries(h, i + 2) for h in heads]
        m_next = [all_scores(h, i + 1, q1[h][0], keep1[h]) for h in heads]
        keep2 = [row_masks(i + 2, q2[h][1]) for h in heads]
        for h in heads:
            attend(h, i, m_row[h])
        m_row, q1, keep1 = m_next, q2, keep2


def _moba(qkv, cos, sin_signed, *, batch, seq, heads, group=2):
    m = qkv.shape[0]
    dh = LANES
    assert heads % group == 0
    ng = heads // group
    blk = (seq, group * dh)
    tab = (seq, dh)
    vmem = (6 * _nbytes(blk, F32) + 4 * _nbytes(tab, F32) + 2 * _nbytes(blk, BF16)
            + 3 * _nbytes(blk, BF16) + group * _nbytes((LANES, dh), F32)
            + 2 * group * _nbytes((MOBA_BLOCK, seq), F32))
    return pl.pallas_call(
        _moba_kernel,
        grid=(batch, ng),
        in_specs=[
            pl.BlockSpec(blk, lambda b, g: (b, g)),
            pl.BlockSpec(blk, lambda b, g: (b, ng + g)),
            pl.BlockSpec(blk, lambda b, g: (b, 2 * ng + g)),
            pl.BlockSpec(tab, lambda b, g: (0, 0)),
            pl.BlockSpec(tab, lambda b, g: (0, 0)),
        ],
        out_specs=pl.BlockSpec(blk, lambda b, g: (b, g)),
        out_shape=jax.ShapeDtypeStruct((m, heads * dh), BF16),
        scratch_shapes=[
            pltpu.VMEM((group, seq, dh), BF16),
            pltpu.VMEM((group, seq, 2 * dh), BF16),
            pltpu.VMEM((group, LANES, dh), F32),
            pltpu.VMEM((2 * group, MOBA_BLOCK, seq), F32),
        ],
        compiler_params=_params(("parallel", "parallel"), vmem),
        name="moba",
    )(qkv, qkv, qkv, cos, sin_signed)


def _gate_merge_kernel(xn_ref, oa_ref, ob_ref, oc_ref, od_ref, wg_ref, bg_ref, wb_ref, y_ref, *, tc):
    xn = xn_ref[...]
    outs = [o_ref[...] for o_ref in (oa_ref, ob_ref, oc_ref, od_ref)]
    for c in range(0, y_ref.shape[1], tc):
        cs = slice(c, c + tc)
        y = None
        for k, o in enumerate(outs):
            gate = _sigmoid(_dot(xn, wg_ref[k, :, cs]) + bg_ref[k, :, cs])
            t = gate * _dot(o, wb_ref[k, :, cs])
            y = t if y is None else y + t
        y_ref[:, cs] = y.astype(BF16)


def _gate_merge(xn, outs, wg, bg, wb, *, tm=1024, slabs=2, tc=MXU_COLS):
    (wg, lg), (wb, lb) = wg, wb
    m, d = xn.shape
    bw = outs[0].shape[1]
    nb = len(outs)
    tn = d // slabs
    vmem = (_nbytes((nb, d, tn), BF16) + _nbytes((nb, bw, tn), BF16)
            + 2 * _nbytes((tm, d), BF16) + 2 * nb * _nbytes((tm, bw), BF16)
            + 2 * _nbytes((tm, tn), BF16) + 6 * _nbytes((tm, tc), F32))
    o_spec = pl.BlockSpec((tm, bw), lambda s, i: (i, 0))
    return pl.pallas_call(
        functools.partial(_gate_merge_kernel, tc=tc),
        grid=(slabs, m // tm),
        in_specs=[
            pl.BlockSpec((tm, d), lambda s, i: (i, 0)),
            o_spec, o_spec, o_spec, o_spec,
            _resident((None, nb, d, tn), lambda s, i: (lg, 0, 0, s)),
            _resident((nb, 1, tn), lambda s, i: (0, 0, s)),
            _resident((None, nb, bw, tn), lambda s, i: (lb, 0, 0, s)),
        ],
        out_specs=pl.BlockSpec((tm, tn), lambda s, i: (i, s)),
        out_shape=jax.ShapeDtypeStruct((m, d), BF16),
        compiler_params=_params(("arbitrary", "arbitrary"), vmem),
        name="gate_merge",
    )(xn, *outs, wg, bg, wb)


def _residual_proj_kernel(x_ref, a_ref, w_ref, g_ref, *refs, final, n_cast):
    cast_refs, out_refs = refs[:n_cast], refs[n_cast:]
    x_new = x_ref[...] + _dot(a_ref[...], w_ref[...])
    if final:
        out_refs[0][...] = _rmsnorm(x_new, g_ref[...])
    else:
        out_refs[0][...] = x_new
        out_refs[1][...] = _rmsnorm(x_new, g_ref[...]).astype(BF16)
    for src, dst in zip(cast_refs, out_refs[len(out_refs) - n_cast:]):
        dst[...] = src[...].astype(BF16)


def _residual_proj(x, a, w, g, *, final, tm, name, casts=()):
    m, d = x.shape
    k = a.shape[1]
    w, l = w
    steps = m // tm
    tile = pl.BlockSpec((tm, d), lambda i: (i, 0))
    vmem = (_nbytes((k, d), BF16) + 2 * _nbytes((tm, k), BF16) + 5 * _nbytes((tm, d), F32)
            + 2 * _nbytes((tm, d), BF16))
    in_specs = [
        tile,
        pl.BlockSpec((tm, k), lambda i: (i, 0)),
        _resident((None, k, d), lambda i: (l, 0, 0)),
        pl.BlockSpec((1, d), lambda i: (0, 0)),
    ]
    x_out = jax.ShapeDtypeStruct((m, d), F32)
    out_specs = [tile] if final else [tile, tile]
    out_shape = [x_out] if final else [x_out, jax.ShapeDtypeStruct((m, d), BF16)]
    for src, layer, per_block in casts:
        _, r, c = src.shape
        rows = r * per_block // steps
        assert rows * steps == r * per_block and rows % (2 * SUBLANES) == 0, (r, steps, per_block)
        in_specs.append(pl.BlockSpec((None, rows, c), lambda i, ly=layer, p=per_block: (ly, i // p, 0)))
        out_specs.append(pl.BlockSpec((rows, c), lambda i, p=per_block: (i // p, 0)))
        out_shape.append(jax.ShapeDtypeStruct((r, c), BF16))
        vmem += 2 * _nbytes((rows, c), F32) + 2 * _nbytes((rows, c), BF16)
    out = pl.pallas_call(
        functools.partial(_residual_proj_kernel, final=final, n_cast=len(casts)),
        grid=(steps,),
        in_specs=in_specs,
        out_specs=out_specs,
        out_shape=out_shape,
        compiler_params=_params(("arbitrary",), vmem),
        name=name,
    )(x, a, w, g, *[src for src, _, _ in casts])
    return out[0] if final else out


def _ffn_up_kernel(xn_ref, w1_ref, w3_ref, h_ref, *, tc):
    xn = xn_ref[...]
    for c in range(0, h_ref.shape[1], tc):
        h1 = _dot(xn, w1_ref[:, c:c + tc])
        h3 = _dot(xn, w3_ref[:, c:c + tc])
        h_ref[:, c:c + tc] = ((h1 * _sigmoid(h1)) * h3).astype(BF16)


def _ffn_up(xn, w1, w3, l, *, tm=1024, slabs=2, tc=MXU_COLS):
    m, d = xn.shape
    f = w1.shape[2]
    tf = f // slabs
    assert tf % tc == 0
    vmem = (2 * _nbytes((d, tf), BF16) + 2 * _nbytes((tm, d), BF16)
            + 2 * _nbytes((tm, tf), BF16) + 6 * _nbytes((tm, tc), F32))
    w_spec = _resident((None, d, tf), lambda s, i: (l, 0, s))
    return pl.pallas_call(
        functools.partial(_ffn_up_kernel, tc=tc),
        grid=(slabs, m // tm),
        in_specs=[pl.BlockSpec((tm, d), lambda s, i: (i, 0)), w_spec, w_spec],
        out_specs=pl.BlockSpec((tm, tf), lambda s, i: (i, s)),
        out_shape=jax.ShapeDtypeStruct((m, f), BF16),
        compiler_params=_params(("arbitrary", "arbitrary"), vmem),
        name="ffn_up",
    )(xn, w1, w3)


def _rope_tables(seq, dh):
    half = dh // 2
    inv = ROPE_THETA ** (-jnp.arange(half, dtype=F32) / half)
    ang = jnp.arange(seq).astype(F32)[:, None] * inv[None, :]
    cos, sin = jnp.cos(ang), jnp.sin(ang)
    return jnp.concatenate([cos, cos], axis=-1), jnp.concatenate([-sin, sin], axis=-1)


def kernel(x, g_mix, w_in, w_sgu, b_sgu, g_sgu, w_sconv, w_lru_conv, b_lru_conv, w_lru_a,
           b_lru_a, w_lru_x, b_lru_x, lru_lambda, w_gate, b_gate, w_branch, w_out, g_ffn,
           w_ffn1, w_ffn3, w_ffn2, g_final):
    batch, seq, d = x.shape
    depth = w_in.shape[0]
    bw = w_branch.shape[2]
    heads = bw // LANES
    assert bw == N_GROUPS * LANES and seq % MOBA_BLOCK == 0 and w_sgu.shape[-1] == CHUNK

    cos, sin_signed = _rope_tables(seq, LANES)
    row = lambda v: v.reshape(1, -1)

    nb = w_gate.shape[1]
    w_branch, w_out = w_branch.astype(BF16), w_out.astype(BF16)
    w_gate = w_gate.reshape(depth, nb * d, d)
    w_in_l, w_gate_l, w_ffn1_l, w_ffn3_l, w_ffn2_l = (
        w[0].astype(BF16) for w in (w_in, w_gate, w_ffn1, w_ffn3, w_ffn2))

    h = x.reshape(batch * seq, d)
    hn = _norm(h, row(g_mix[0]))
    for l in range(depth):
        last = l == depth - 1
        qkv, o_a, o_b, o_d = _in_proj_mix(
            hn, w_in_l[None], 0,
            (w_sgu[l], b_sgu[l].T[:, :, None], row(g_sgu[l])),
            w_sconv[l],
            (w_lru_conv[l], row(b_lru_conv[l]), w_lru_a[l].astype(BF16), row(b_lru_a[l]),
             w_lru_x[l].astype(BF16), row(b_lru_x[l]), row(lru_lambda[l])),
            seq=seq, bw=bw)
        o_c = _moba(qkv, cos, sin_signed, batch=batch, seq=seq, heads=heads)
        y = _gate_merge(hn, (o_a, o_b, o_c, o_d), (w_gate_l.reshape(1, nb, d, d), 0),
                        b_gate[l][:, None, :], (w_branch, l))
        if last:
            h, hn = _residual_proj(h, y, (w_out, l), row(g_ffn[l]), final=False,
                                   tm=512, name="out_proj_last")
            hid = _ffn_up(hn, w_ffn1_l[None], w_ffn3_l[None], 0)
            h = _residual_proj(h, hid, (w_ffn2_l[None], 0), row(g_final), final=True,
                               tm=256, name="ffn_down_final")
        else:
            h, hn, w_in_l, w_gate_l, w_ffn2_next = _residual_proj(
                h, y, (w_out, l), row(g_ffn[l]), final=False, tm=512, name="out_proj",
                casts=((w_in, l + 1, 1), (w_gate, l + 1, 1), (w_ffn2, l + 1, 2)))
            hid = _ffn_up(hn, w_ffn1_l[None], w_ffn3_l[None], 0)
            h, hn, w_ffn1_l, w_ffn3_l = _residual_proj(
                h, hid, (w_ffn2_l[None], 0), row(g_mix[l + 1]), final=False, tm=256,
                name="ffn_down", casts=((w_ffn1, l + 1, 1), (w_ffn3, l + 1, 1)))
            w_ffn2_l = w_ffn2_next
    return h.reshape(batch, seq, d)
```

```python
import functools

import jax
import jax.numpy as jnp
import numpy as np
from jax import lax
from jax.experimental import pallas as pl
from jax.experimental.pallas import tpu as pltpu

N_GROUPS = 4
CHUNK = 128
SCONV_W = 3
MOBA_BLOCK = 256
MOBA_TOPK = 3
ROPE_THETA = 10000.0
NEG = -1e30
LRU_CONV_W = 4
LRU_C = 8.0
EPS = 1e-6

LANES = 128
SUBLANES = 8
MXU_COLS = 256
V7X_VMEM_BYTES = 64 * 1024 * 1024
VMEM_BUDGET_BYTES = V7X_VMEM_BYTES - 8 * 1024 * 1024

F32 = jnp.float32
BF16 = jnp.bfloat16

_SQRT_2_OVER_PI = float(np.sqrt(2.0 / np.pi).astype(np.float32))
_LOG2_E = 1.4426950408889634


def _params(semantics, vmem_bytes):
    assert vmem_bytes <= VMEM_BUDGET_BYTES, vmem_bytes
    return pltpu.CompilerParams(dimension_semantics=semantics,
                                vmem_limit_bytes=VMEM_BUDGET_BYTES)


def _nbytes(shape, dtype):
    return int(np.prod(shape)) * jnp.dtype(dtype).itemsize


def _resident(shape, index_map):
    return pl.BlockSpec(shape, index_map, pipeline_mode=pl.Buffered(1))


def _rmsnorm(x, g):
    var = jnp.mean(x * x, axis=-1, keepdims=True)
    return x * lax.rsqrt(var + EPS) * g


def _gelu_tanh(x):
    return x * (0.5 * (1.0 + jnp.tanh(_SQRT_2_OVER_PI * (x + 0.044715 * (x * x * x)))))


def _sigmoid(x):
    return 0.5 * (jnp.tanh(0.5 * x) + 1.0)


def _softplus(x):
    return jnp.maximum(x, 0.0) + jnp.log1p(jnp.exp(-jnp.abs(x)))


def _shift_rows(x, d, halo):
    row = lax.broadcasted_iota(jnp.int32, x.shape, 0)
    pad = jnp.zeros((x.shape[0] - SUBLANES, x.shape[1]), x.dtype)
    fill = jnp.concatenate([pltpu.roll(halo, d, axis=0), pad], axis=0)
    return jnp.where(row >= d, pltpu.roll(x, d, axis=0), fill)


def _causal_conv(x, w_ref, width, halo):
    acc = None
    for k in range(width):
        d = width - 1 - k
        term = (_shift_rows(x, d, halo) if d else x) * w_ref[k:k + 1, :]
        acc = term if acc is None else acc + term
    return acc


def _dot(a, b):
    return jnp.dot(a, b, preferred_element_type=F32)


def _dot_nt(a, b):
    return lax.dot_general(a, b, (((1,), (1,)), ((), ())), preferred_element_type=F32)


def _split_bf16(x):
    hi = x.astype(BF16)
    lo = (x - hi.astype(F32)).astype(BF16)
    return hi, lo


def _in_proj_mix_kernel(*refs, tiles_per_seq, norm_input):
    if norm_input:
        x_ref, g_ref, *refs = refs
    else:
        xn_ref, *refs = refs
    (w_ref, wsgu_ref, bsgu_ref, gsgu_ref, wsc_ref,
     cw_ref, cb_ref, wa_ref, ba_ref, wx_ref, bx_ref, lam_ref,
     qkv_ref, oa_ref, ob_ref, od_ref, *refs) = refs
    if norm_input:
        xn_out_ref, *refs = refs
    zprev_ref, xprev_ref, hcarry_ref, a_ref, h_ref = refs
    tm, bw = oa_ref.shape

    @pl.when(pl.program_id(0) % tiles_per_seq == 0)
    def _():
        zprev_ref[...] = jnp.zeros(zprev_ref.shape, F32)
        xprev_ref[...] = jnp.zeros(xprev_ref.shape, F32)
        hcarry_ref[...] = jnp.zeros(hcarry_ref.shape, F32)

    if norm_input:
        xn = _rmsnorm(x_ref[...], g_ref[...]).astype(BF16)
        xn_out_ref[...] = xn
    else:
        xn = xn_ref[...]

    def proj(c):
        return _dot(xn, w_ref[:, c * bw:(c + 1) * bw])

    groups = [slice(g * LANES, (g + 1) * LANES) for g in range(N_GROUPS)]

    xr, gate_in = proj(8), proj(9)

    xhalo = xprev_ref[...]
    xprev_ref[...] = xr[tm - SUBLANES:, :]
    xc = _causal_conv(xr, cw_ref, LRU_CONV_W, xhalo) + cb_ref[...]
    xb = xc.astype(BF16)

    u, v = proj(0), proj(1)

    r_pre = jnp.concatenate([_dot(xb[:, cs], wa_ref[g]) for g, cs in enumerate(groups)], axis=1)
    i_pre = jnp.concatenate([_dot(xb[:, cs], wx_ref[g]) for g, cs in enumerate(groups)], axis=1)

    vn = _rmsnorm(_gelu_tanh(v), gsgu_ref[...]).astype(BF16)
    gu = _gelu_tanh(u)

    for c0 in range(0, bw, MXU_COLS):
        cs = slice(c0, c0 + MXU_COLS)
        c_gate, x_conv, b_gate = (_dot(xn, w_ref[:, c * bw + c0:c * bw + c0 + MXU_COLS])
                                  for c in (3, 4, 2))
        z = c_gate * x_conv
        zhalo = zprev_ref[:, cs]
        zprev_ref[:, cs] = z[tm - SUBLANES:, :]
        ob_ref[:, cs] = (b_gate * _causal_conv(z, wsc_ref.at[:, cs], SCONV_W, zhalo)).astype(BF16)

    row = lax.broadcasted_iota(jnp.int32, (CHUNK, CHUNK), 0)
    col = lax.broadcasted_iota(jnp.int32, (CHUNK, CHUNK), 1)
    for g, cs in enumerate(groups):
        w = jnp.where(col <= row, wsgu_ref[g], 0.0).astype(BF16)
        bias = bsgu_ref[g]
        for n in range(tm // CHUNK):
            rs = slice(n * CHUNK, (n + 1) * CHUNK)
            oa_ref[rs, cs] = (gu[rs, cs] * (_dot(w, vn[rs, cs]) + bias)).astype(BF16)

    r = _sigmoid(r_pre + ba_ref[...])
    i = _sigmoid(i_pre + bx_ref[...])
    log_a = (-LRU_C * r) * _softplus(-lam_ref[...])
    a = jnp.exp(log_a)
    b = jnp.sqrt(jnp.tanh(-log_a) * (1.0 + a * a)) * (i * xc)

    qkv_ref[:, 0:bw] = proj(5)

    seg_len = tm // SUBLANES
    seg_stride = a_ref.shape[1] // SUBLANES
    for g, cs in enumerate(groups):
        for s in range(SUBLANES):
            dst = slice(s * seg_stride, s * seg_stride + seg_len)
            a_ref[g, dst, :] = a[s * seg_len:(s + 1) * seg_len, cs]
            h_ref[g, dst, :] = b[s * seg_len:(s + 1) * seg_len, cs]
    h = [jnp.zeros((SUBLANES, LANES), F32)] * N_GROUPS
    prod = [jnp.ones((SUBLANES, LANES), F32)] * N_GROUPS
    for t in range(seg_len):
        rows = pl.ds(t, SUBLANES, stride=seg_stride)
        for g in range(N_GROUPS):
            at = a_ref[g, rows, :]
            h[g] = at * h[g] + h_ref[g, rows, :]
            prod[g] = at * prod[g]
            h_ref[g, rows, :] = h[g]
            a_ref[g, rows, :] = prod[g]

    qkv_ref[:, bw:2 * bw] = proj(6)

    gate = _gelu_tanh(gate_in)
    for g, cs in enumerate(groups):
        carry = hcarry_ref[0:1, cs]
        for s in range(SUBLANES):
            seg = slice(s * seg_stride, s * seg_stride + seg_len)
            out = slice(s * seg_len, (s + 1) * seg_len)
            hs = h_ref[g, seg, :] + a_ref[g, seg, :] * carry
            od_ref[out, cs] = (gate[out, cs] * hs).astype(BF16)
            carry = h[g][s:s + 1, :] + prod[g][s:s + 1, :] * carry
        hcarry_ref[0:1, cs] = carry

    qkv_ref[:, 2 * bw:3 * bw] = proj(7)


def _in_proj_mix(xn, w, l, sgu, sconv_w, lru, *, seq, bw, tm=512, norm_gain=None):
    m, d = xn.shape
    n = w.shape[2]
    norm_input = norm_gain is not None
    assert seq % tm == 0 and n == 10 * bw
    seg_stride = tm // SUBLANES + SUBLANES
    scan_buf = (bw // LANES, SUBLANES * seg_stride, LANES)
    halo = (SUBLANES, bw)
    rows = pl.BlockSpec((tm, d), lambda i: (i, 0))
    vmem = (_nbytes((d, n), BF16) + 2 * _nbytes((tm, d), xn.dtype) + 2 * _nbytes((tm, 3 * bw), F32)
            + 6 * _nbytes((tm, bw), BF16) + 2 * _nbytes(scan_buf, F32) + 8 * _nbytes((tm, bw), F32)
            + norm_input * 2 * _nbytes((tm, d), BF16))
    full = lambda a: pl.BlockSpec(a.shape, lambda i, nd=a.ndim: (0,) * nd)
    small = (*sgu, sconv_w, *lru)
    o_spec = pl.BlockSpec((tm, bw), lambda i: (i, 0))
    o_shape = jax.ShapeDtypeStruct((m, bw), BF16)
    head = (xn, norm_gain) if norm_input else (xn,)
    return pl.pallas_call(
        functools.partial(_in_proj_mix_kernel, tiles_per_seq=seq // tm, norm_input=norm_input),
        grid=(m // tm,),
        in_specs=[
            rows,
            *([full(norm_gain)] if norm_input else []),
            _resident((None, d, n), lambda i: (l, 0, 0)),
            *[full(a) for a in small],
        ],
        out_specs=(pl.BlockSpec((tm, 3 * bw), lambda i: (i, 0)), o_spec, o_spec, o_spec,
                   *([rows] if norm_input else [])),
        out_shape=(jax.ShapeDtypeStruct((m, 3 * bw), F32), o_shape, o_shape, o_shape,
                   *([jax.ShapeDtypeStruct((m, d), BF16)] if norm_input else [])),
        scratch_shapes=[pltpu.VMEM(halo, F32), pltpu.VMEM(halo, F32), pltpu.VMEM(halo, F32),
                        pltpu.VMEM(scan_buf, F32), pltpu.VMEM(scan_buf, F32)],
        compiler_params=_params(("arbitrary",), vmem),
        name="in_proj_mix",
    )(*head, w, *small)


def _rope(x, cos, sin_signed):
    return x * cos + pltpu.roll(x, x.shape[1] // 2, axis=1) * sin_signed


def _moba_kernel(q_ref, k_ref, v_ref, cos_ref, sin_ref, o_ref,
                 kr_ref, va_ref, km_ref, s_ref):
    seq = q_ref.shape[0]
    dh = LANES
    heads = range(q_ref.shape[1] // dh)
    lanes = [slice(h * dh, (h + 1) * dh) for h in heads]
    blk = MOBA_BLOCK
    nblk = seq // blk
    sub = SUBLANES
    assert nblk <= sub and blk == 2 * dh
    q_scale = dh ** -0.5 * _LOG2_E

    km_ref[...] = jnp.zeros(km_ref.shape, F32)
    for n in range(nblk):
        rs = slice(n * blk, (n + 1) * blk)
        for h in heads:
            kr = _rope(k_ref[rs, lanes[h]], cos_ref[rs, :], sin_ref[rs, :])
            kr_ref[h, rs, :] = kr.astype(BF16)
            km_ref[h, n:n + 1, :] = jnp.sum(kr, axis=0, keepdims=True) * (1.0 / blk)
            va_ref[h, rs, :dh] = v_ref[rs, lanes[h]].astype(BF16)
            va_ref[h, rs, dh:] = jnp.ones((blk, dh), BF16)
    km_parts = []
    for h in heads:
        km_hi, km_lo = _split_bf16(km_ref[h])
        km_parts.append((jnp.concatenate([km_hi, km_hi], axis=1), km_lo))

    row = lax.broadcasted_iota(jnp.int32, (blk, blk), 0)
    col = lax.broadcasted_iota(jnp.int32, (blk, blk), 1)
    causal = col <= row
    blk_id = lax.broadcasted_iota(jnp.int32, (sub, blk), 0)
    er = lax.broadcasted_iota(jnp.int32, (LANES, nblk * dh), 0)
    ec = lax.broadcasted_iota(jnp.int32, (LANES, nblk * dh), 1)
    expand = jnp.where(er == ec // dh, 1.0, 0.0).astype(BF16)

    def rotary_queries(h, i):
        if i >= nblk:
            return None, None
        rs = slice(i * blk, (i + 1) * blk)
        qr = _rope(q_ref[rs, lanes[h]], cos_ref[rs, :], sin_ref[rs, :])
        qs = (qr * q_scale).astype(BF16)
        if i <= MOBA_TOPK:
            return qs, None
        q_hi, q_lo = _split_bf16(qr)
        q_hl = jnp.concatenate([q_hi, q_lo], axis=1)
        km_hh, km_lo = km_parts[h]
        return qs, (_dot_nt(km_hh, q_hl) + _dot_nt(km_lo, q_hi))[:sub, :]

    def row_masks(i, gate):
        if gate is None:
            return None
        cnt = jnp.zeros(gate.shape, F32)
        for m in range(i):
            gm = gate[m:m + 1, :]
            beats = (gm > gate) | ((gm == gate) & (blk_id > m))
            cnt = cnt + jnp.where(beats, 1.0, 0.0)
        sel_t = jnp.where(cnt < float(MOBA_TOPK), 1.0, 0.0)
        sel_t = jnp.concatenate([sel_t, jnp.zeros((LANES - sub, blk), F32)], axis=0)
        sel = sel_t.T.astype(BF16)
        keep = [None] * i
        for n in range(0, i, 2):
            u = _dot(sel, expand[:, n * dh:(n + 2) * dh])
            keep[n] = jnp.concatenate([u[:, :dh], u[:, :dh]], axis=1) > 0.5
            if n + 1 < i:
                keep[n + 1] = jnp.concatenate([u[:, dh:], u[:, dh:]], axis=1) > 0.5
        return keep

    def all_scores(h, i, qs, keep):
        if i >= nblk:
            return None
        slot = 2 * h + i % 2
        m_tile = None
        for j in range(i + 1):
            ks = slice(j * blk, (j + 1) * blk)
            s = _dot_nt(qs, kr_ref[h, ks, :])
            if j == i:
                s = jnp.where(causal, s, NEG)
            elif keep is not None:
                s = jnp.where(keep[j], s, NEG)
            s_ref[slot, :, ks] = s
            mj = jnp.maximum(s[:, :dh], s[:, dh:])
            m_tile = mj if m_tile is None else jnp.maximum(m_tile, mj)
        return jnp.max(m_tile, axis=-1, keepdims=True)

    def attend(h, i, m_row):
        slot = 2 * h + i % 2
        acc = jnp.zeros((blk, 2 * dh), F32)
        for j in range(i + 1):
            ks = slice(j * blk, (j + 1) * blk)
            p = jnp.exp2(s_ref[slot, :, ks] - m_row)
            acc = acc + _dot(p.astype(BF16), va_ref[h, ks, :])
        o_ref[i * blk:(i + 1) * blk, lanes[h]] = (acc[:, :dh] / acc[:, dh:]).astype(BF16)

    q0 = [rotary_queries(h, 0) for h in heads]
    m_row = [all_scores(h, 0, q0[h][0], row_masks(0, q0[h][1])) for h in heads]
    q1 = [rotary_queries(h, 1) for h in heads]
    keep1 = [row_masks(1, q1[h][1]) for h in heads]
    for i in range(nblk):
        q2 = [rotary_queries(h, i + 2) for h in heads]
        m_next = [all_scores(h, i + 1, q1[h][0], keep1[h]) for h in heads]
        keep2 = [row_masks(i + 2, q2[h][1]) for h in heads]
        for h in heads:
            attend(h, i, m_row[h])
        m_row, q1, keep1 = m_next, q2, keep2


def _moba(qkv, cos, sin_signed, *, batch, seq, heads, group=2):
    m = qkv.shape[0]
    dh = LANES
    assert heads % group == 0
    ng = heads // group
    blk = (seq, group * dh)
    tab = (seq, dh)
    vmem = (6 * _nbytes(blk, F32) + 4 * _nbytes(tab, F32) + 2 * _nbytes(blk, BF16)
            + 3 * _nbytes(blk, BF16) + group * _nbytes((LANES, dh), F32)
            + 2 * group * _nbytes((MOBA_BLOCK, seq), F32))
    return pl.pallas_call(
        _moba_kernel,
        grid=(batch, ng),
        in_specs=[
            pl.BlockSpec(blk, lambda b, g: (b, g)),
            pl.BlockSpec(blk, lambda b, g: (b, ng + g)),
            pl.BlockSpec(blk, lambda b, g: (b, 2 * ng + g)),
            pl.BlockSpec(tab, lambda b, g: (0, 0)),
            pl.BlockSpec(tab, lambda b, g: (0, 0)),
        ],
        out_specs=pl.BlockSpec(blk, lambda b, g: (b, g)),
        out_shape=jax.ShapeDtypeStruct((m, heads * dh), BF16),
        scratch_shapes=[
            pltpu.VMEM((group, seq, dh), BF16),
            pltpu.VMEM((group, seq, 2 * dh), BF16),
            pltpu.VMEM((group, LANES, dh), F32),
            pltpu.VMEM((2 * group, MOBA_BLOCK, seq), F32),
        ],
        compiler_params=_params(("parallel", "parallel"), vmem),
        name="moba",
    )(qkv, qkv, qkv, cos, sin_signed)


def _gate_merge_kernel(xn_ref, oa_ref, ob_ref, oc_ref, od_ref, wg_ref, bg_ref, wb_ref, y_ref, *, tc):
    xn = xn_ref[...]
    outs = [o_ref[...] for o_ref in (oa_ref, ob_ref, oc_ref, od_ref)]
    for c in range(0, y_ref.shape[1], tc):
        cs = slice(c, c + tc)
        y = None
        for k, o in enumerate(outs):
            gate = _sigmoid(_dot(xn, wg_ref[k, :, cs]) + bg_ref[k, :, cs])
            t = gate * _dot(o, wb_ref[k, :, cs])
            y = t if y is None else y + t
        y_ref[:, cs] = y.astype(BF16)


def _gate_merge(xn, outs, wg, bg, wb, *, tm=1024, slabs=2, tc=MXU_COLS):
    (wg, lg), (wb, lb) = wg, wb
    m, d = xn.shape
    bw = outs[0].shape[1]
    nb = len(outs)
    tn = d // slabs
    vmem = (_nbytes((nb, d, tn), BF16) + _nbytes((nb, bw, tn), BF16)
            + 2 * _nbytes((tm, d), BF16) + 2 * nb * _nbytes((tm, bw), BF16)
            + 2 * _nbytes((tm, tn), BF16) + 6 * _nbytes((tm, tc), F32))
    o_spec = pl.BlockSpec((tm, bw), lambda s, i: (i, 0))
    return pl.pallas_call(
        functools.partial(_gate_merge_kernel, tc=tc),
        grid=(slabs, m // tm),
        in_specs=[
            pl.BlockSpec((tm, d), lambda s, i: (i, 0)),
            o_spec, o_spec, o_spec, o_spec,
            _resident((None, nb, d, tn), lambda s, i: (lg, 0, 0, s)),
            _resident((nb, 1, tn), lambda s, i: (0, 0, s)),
            _resident((None, nb, bw, tn), lambda s, i: (lb, 0, 0, s)),
        ],
        out_specs=pl.BlockSpec((tm, tn), lambda s, i: (i, s)),
        out_shape=jax.ShapeDtypeStruct((m, d), BF16),
        compiler_params=_params(("arbitrary", "arbitrary"), vmem),
        name="gate_merge",
    )(xn, *outs, wg, bg, wb)


def _residual_proj_kernel(x_ref, a_ref, w_ref, g_ref, *refs, final, n_cast):
    cast_refs, out_refs = refs[:n_cast], refs[n_cast:]
    x_new = x_ref[...] + _dot(a_ref[...], w_ref[...])
    if final:
        out_refs[0][...] = _rmsnorm(x_new, g_ref[...])
    else:
        out_refs[0][...] = x_new
        out_refs[1][...] = _rmsnorm(x_new, g_ref[...]).astype(BF16)
    for src, dst in zip(cast_refs, out_refs[len(out_refs) - n_cast:]):
        dst[...] = src[...].astype(BF16)


def _residual_proj(x, a, w, g, *, final, tm, name, casts=()):
    m, d = x.shape
    k = a.shape[1]
    w, l = w
    steps = m // tm
    tile = pl.BlockSpec((tm, d), lambda i: (i, 0))
    vmem = (_nbytes((k, d), BF16) + 2 * _nbytes((tm, k), BF16) + 5 * _nbytes((tm, d), F32)
            + 2 * _nbytes((tm, d), BF16))
    in_specs = [
        tile,
        pl.BlockSpec((tm, k), lambda i: (i, 0)),
        _resident((None, k, d), lambda i: (l, 0, 0)),
        pl.BlockSpec((1, d), lambda i: (0, 0)),
    ]
    x_out = jax.ShapeDtypeStruct((m, d), F32)
    out_specs = [tile] if final else [tile, tile]
    out_shape = [x_out] if final else [x_out, jax.ShapeDtypeStruct((m, d), BF16)]
    for src, layer, per_block in casts:
        _, r, c = src.shape
        rows = r * per_block // steps
        assert rows * steps == r * per_block and rows % (2 * SUBLANES) == 0, (r, steps, per_block)
        in_specs.append(pl.BlockSpec((None, rows, c), lambda i, ly=layer, p=per_block: (ly, i // p, 0)))
        out_specs.append(pl.BlockSpec((rows, c), lambda i, p=per_block: (i // p, 0)))
        out_shape.append(jax.ShapeDtypeStruct((r, c), BF16))
        vmem += 2 * _nbytes((rows, c), F32) + 2 * _nbytes((rows, c), BF16)
    out = pl.pallas_call(
        functools.partial(_residual_proj_kernel, final=final, n_cast=len(casts)),
        grid=(steps,),
        in_specs=in_specs,
        out_specs=out_specs,
        out_shape=out_shape,
        compiler_params=_params(("arbitrary",), vmem),
        name=name,
    )(x, a, w, g, *[src for src, _, _ in casts])
    return out[0] if final else out


def _ffn_up_kernel(xn_ref, w1_ref, w3_ref, h_ref, *, tc):
    xn = xn_ref[...]
    for c in range(0, h_ref.shape[1], tc):
        h1 = _dot(xn, w1_ref[:, c:c + tc])
        h3 = _dot(xn, w3_ref[:, c:c + tc])
        h_ref[:, c:c + tc] = ((h1 * _sigmoid(h1)) * h3).astype(BF16)


def _ffn_up(xn, w1, w3, l, *, tm=1024, slabs=2, tc=MXU_COLS):
    m, d = xn.shape
    f = w1.shape[2]
    tf = f // slabs
    assert tf % tc == 0
    vmem = (2 * _nbytes((d, tf), BF16) + 2 * _nbytes((tm, d), BF16)
            + 2 * _nbytes((tm, tf), BF16) + 6 * _nbytes((tm, tc), F32))
    w_spec = _resident((None, d, tf), lambda s, i: (l, 0, s))
    return pl.pallas_call(
        functools.partial(_ffn_up_kernel, tc=tc),
        grid=(slabs, m // tm),
        in_specs=[pl.BlockSpec((tm, d), lambda s, i: (i, 0)), w_spec, w_spec],
        out_specs=pl.BlockSpec((tm, tf), lambda s, i: (i, s)),
        out_shape=jax.ShapeDtypeStruct((m, f), BF16),
        compiler_params=_params(("arbitrary", "arbitrary"), vmem),
        name="ffn_up",
    )(xn, w1, w3)


def _rope_tables(seq, dh):
    half = dh // 2
    inv = ROPE_THETA ** (-jnp.arange(half, dtype=F32) / half)
    ang = jnp.arange(seq).astype(F32)[:, None] * inv[None, :]
    cos, sin = jnp.cos(ang), jnp.sin(ang)
    return jnp.concatenate([cos, cos], axis=-1), jnp.concatenate([-sin, sin], axis=-1)


def kernel(x, g_mix, w_in, w_sgu, b_sgu, g_sgu, w_sconv, w_lru_conv, b_lru_conv, w_lru_a,
           b_lru_a, w_lru_x, b_lru_x, lru_lambda, w_gate, b_gate, w_branch, w_out, g_ffn,
           w_ffn1, w_ffn3, w_ffn2, g_final):
    batch, seq, d = x.shape
    depth = w_in.shape[0]
    bw = w_branch.shape[2]
    heads = bw // LANES
    assert bw == N_GROUPS * LANES and seq % MOBA_BLOCK == 0 and w_sgu.shape[-1] == CHUNK

    cos, sin_signed = _rope_tables(seq, LANES)
    row = lambda v: v.reshape(1, -1)

    nb = w_gate.shape[1]
    w_branch, w_out = w_branch.astype(BF16), w_out.astype(BF16)
    w_gate = w_gate.reshape(depth, nb * d, d)
    w_in_l, w_gate_l, w_ffn1_l, w_ffn3_l, w_ffn2_l = (
        w[0].astype(BF16) for w in (w_in, w_gate, w_ffn1, w_ffn3, w_ffn2))

    h = x.reshape(batch * seq, d)
    hn = None
    for l in range(depth):
        last = l == depth - 1
        qkv, o_a, o_b, o_d, *rest = _in_proj_mix(
            h if l == 0 else hn, w_in_l[None], 0,
            (w_sgu[l], b_sgu[l].T[:, :, None], row(g_sgu[l])),
            w_sconv[l],
            (w_lru_conv[l], row(b_lru_conv[l]), w_lru_a[l].astype(BF16), row(b_lru_a[l]),
             w_lru_x[l].astype(BF16), row(b_lru_x[l]), row(lru_lambda[l])),
            seq=seq, bw=bw, norm_gain=row(g_mix[0]) if l == 0 else None)
        if l == 0:
            (hn,) = rest
        o_c = _moba(qkv, cos, sin_signed, batch=batch, seq=seq, heads=heads)
        y = _gate_merge(hn, (o_a, o_b, o_c, o_d), (w_gate_l.reshape(1, nb, d, d), 0),
                        b_gate[l][:, None, :], (w_branch, l))
        if last:
            h, hn = _residual_proj(h, y, (w_out, l), row(g_ffn[l]), final=False,
                                   tm=512, name="out_proj_last")
            hid = _ffn_up(hn, w_ffn1_l[None], w_ffn3_l[None], 0)
            h = _residual_proj(h, hid, (w_ffn2_l[None], 0), row(g_final), final=True,
                               tm=256, name="ffn_down_final")
        else:
            h, hn, w_ffn2_next = _residual_proj(
                h, y, (w_out, l), row(g_ffn[l]), final=False, tm=512, name="out_proj",
                casts=((w_ffn2, l + 1, 2),))
            hid = _ffn_up(hn, w_ffn1_l[None], w_ffn3_l[None], 0)
            h, hn, w_in_l, w_gate_l, w_ffn1_l, w_ffn3_l = _residual_proj(
                h, hid, (w_ffn2_l[None], 0), row(g_mix[l + 1]), final=False, tm=256,
                name="ffn_down",
                casts=((w_in, l + 1, 1), (w_gate, l + 1, 1), (w_ffn1, l + 1, 1), (w_ffn3, l + 1, 1)))
            w_ffn2_l = w_ffn2_next
    return h.reshape(batch, seq, d)
```
